```python
import jax, jax.numpy as jnp
from jax import lax
import numpy as np

D_MODEL = 1024
BATCH = 4
SEQ = 4096
DEPTH = 4
DEC_BATCH = 128
DEC_SEQ = 1
PAST_LEN = 2048
PAGE_SIZE = 128

HEAD_DIM = 64
N_BRANCH = 4
HEADS_PER_MIXER = D_MODEL // (N_BRANCH * HEAD_DIM)
BRANCH_WIDTH = HEADS_PER_MIXER * HEAD_DIM
N_Q_HEADS = N_BRANCH * HEADS_PER_MIXER
SB_Q0 = 0
NSA_Q0 = HEADS_PER_MIXER
MOBA_Q0 = 2 * HEADS_PER_MIXER
FOX_Q0 = 3 * HEADS_PER_MIXER
SB_KV0 = 0
CMP_KV = HEADS_PER_MIXER
SEL_KV = HEADS_PER_MIXER + 1
MOBA_KV0 = HEADS_PER_MIXER + 2
FOX_KV0 = 2 * HEADS_PER_MIXER + 2
N_KV_HEADS = 3 * HEADS_PER_MIXER + 2
CMP_STRIDE = 16
CMP_LEN = 2 * CMP_STRIDE
CMP_HIDDEN = 2 * HEAD_DIM
SEL_BLOCK = 64
N_SEL = 16
N_LOCAL_SEL = 2
WINDOW = 512
N_NSA_BRANCH = 3
MOBA_BLOCK = 256
MOBA_TOPK = 3
Q_BLOCK = 128
D_FF = ((8 * D_MODEL // 3 + 127) // 128) * 128
RMS_EPS = 1e-6
Q_COLS = N_Q_HEADS * HEAD_DIM
KV_COLS = N_KV_HEADS * HEAD_DIM
GATE_COLS = HEADS_PER_MIXER * N_NSA_BRANCH
FGT_COLS = HEADS_PER_MIXER
IN_COLS = Q_COLS + 2 * KV_COLS + 2 * HEAD_DIM + GATE_COLS + FGT_COLS

kernel_name = 'hybrid_sb_nsa_moba_fox_macaron_step'


def rms_norm(x, g):
    xf = x.astype(jnp.float32)
    y = xf * lax.rsqrt(jnp.mean(xf * xf, axis=-1, keepdims=True) + RMS_EPS)
    return (y * g.astype(jnp.float32)).astype(x.dtype)


def masked_softmax(logits, mask, axis=-1):
    logits = jnp.where(mask, logits, -jnp.inf)
    m = jnp.max(logits, axis=axis, keepdims=True)
    m = jnp.where(jnp.isfinite(m), m, 0.0)
    p = jnp.exp(logits - m)
    return p / jnp.maximum(jnp.sum(p, axis=axis, keepdims=True), 1e-30)


def alibi_slopes(n):
    return 2.0 ** (-8.0 * jnp.arange(1, n + 1, dtype=jnp.float32) / n)


def head_group(x, start, n):
    return x[:, :, start:start + n]


def swiglu_half(x, g, w_gate, w_up, w_down):
    h = rms_norm(x, g)
    return 0.5 * ((jax.nn.silu(h @ w_gate) * (h @ w_up)) @ w_down)


def blockify(x, blk):
    L = x.shape[1]
    n = -(-L // blk)
    x = jnp.pad(x, [(0, 0), (0, n * blk - L)] + [(0, 0)] * (x.ndim - 2))
    return x.reshape(x.shape[0], n, blk, *x.shape[2:])


def gather_pages(pool, page_table):
    rows = pool[page_table]
    return rows.reshape(page_table.shape[0], -1, *pool.shape[2:])


def project_inputs(h, w_in, b_fgt):
    B, L, _ = h.shape
    z = h @ w_in
    cuts = [Q_COLS, Q_COLS + KV_COLS, Q_COLS + 2 * KV_COLS, Q_COLS + 2 * KV_COLS + HEAD_DIM,
            Q_COLS + 2 * KV_COLS + 2 * HEAD_DIM, IN_COLS - FGT_COLS]
    q, k, v, kw, vw, g, f = jnp.split(z, cuts, axis=-1)
    q = q.reshape(B, L, N_Q_HEADS, HEAD_DIM)
    k = k.reshape(B, L, N_KV_HEADS, HEAD_DIM)
    v = v.reshape(B, L, N_KV_HEADS, HEAD_DIM)
    kw = kw.reshape(B, L, 1, HEAD_DIM)
    vw = vw.reshape(B, L, 1, HEAD_DIM)
    g = g.reshape(B, L, HEADS_PER_MIXER, N_NSA_BRANCH)
    logf = jax.nn.log_sigmoid(f.astype(jnp.float32) + b_fgt.astype(jnp.float32))
    return q, k, v, kw, vw, g, logf


def nsa_compress(x, pos_emb, w1, w2):
    B, L, D = x.shape
    n16 = L // CMP_STRIDE
    xs = x[:, :n16 * CMP_STRIDE].reshape(B, n16, CMP_STRIDE, D)
    blocks = jnp.concatenate([xs[:, :-1], xs[:, 1:]], axis=2) + pos_emb
    hid = jax.nn.gelu(blocks.reshape(B, n16 - 1, CMP_LEN * D) @ w1)
    return hid @ w2


def prepare_keys(k, v, cum, cmp):
    pos_k, w1k, w2k, pos_v, w1v, w2v = cmp
    kc = nsa_compress(k[:, :, CMP_KV], pos_k, w1k, w2k)
    vc = nsa_compress(v[:, :, CMP_KV], pos_v, w1v, w2v)
    sel_kb = blockify(k[:, :, SEL_KV], SEL_BLOCK)
    sel_vb = blockify(v[:, :, SEL_KV], SEL_BLOCK)
    moba_kb = blockify(head_group(k, MOBA_KV0, HEADS_PER_MIXER), MOBA_BLOCK).transpose(0, 3, 1, 2, 4)
    moba_vb = blockify(head_group(v, MOBA_KV0, HEADS_PER_MIXER), MOBA_BLOCK).transpose(0, 3, 1, 2, 4)
    moba_kmean = jnp.mean(moba_kb.astype(jnp.float32), axis=3)
    return (k, v, cum, kc, vc, sel_kb, sel_vb, moba_kb, moba_vb, moba_kmean)


def stick_breaking_attention(q, k, v, q_pos, k_pos):
    z = jnp.einsum('bqhd,bkhd->bhqk', q, k).astype(jnp.float32) * (HEAD_DIM ** -0.5)
    before = k_pos[None, :] < q_pos[:, None]
    log_keep = jnp.where(before, jax.nn.log_sigmoid(-z), 0.0)
    later = lax.cumsum(log_keep, axis=3, reverse=True) - log_keep
    w = jnp.where(before, jnp.exp(jax.nn.log_sigmoid(z) + later), 0.0)
    return jnp.einsum('bhqk,bkhd->bqhd', w.astype(v.dtype), v)


def forgetting_attention(q, k, v, cum_q, cum_k, q_pos, k_pos):
    s = jnp.einsum('bqhd,bkhd->bhqk', q, k).astype(jnp.float32) * (HEAD_DIM ** -0.5)
    s = s + jnp.transpose(cum_q, (0, 2, 1))[..., None] - jnp.transpose(cum_k, (0, 2, 1))[:, :, None, :]
    p = masked_softmax(s, k_pos[None, :] <= q_pos[:, None])
    return jnp.einsum('bhqk,bkhd->bqhd', p.astype(v.dtype), v)


def nsa_attention(q, gates, q_pos, kc, vc, sel_kb, sel_vb, win_k, win_v, win_pos, slopes):
    B, Q, H, D = q.shape
    NC = kc.shape[1]
    NSB = sel_kb.shape[1]
    scale = HEAD_DIM ** -0.5
    f32 = jnp.float32
    c_end = jnp.arange(NC) * CMP_STRIDE + (CMP_LEN - 1)
    dist_c = q_pos[:, None] - c_end[None, :]
    s_c = jnp.einsum('bqhd,bnd->bhqn', q, kc).astype(f32) * scale - slopes[:, None, None] * dist_c.astype(f32)
    p_c = masked_softmax(s_c, dist_c >= 0)
    o_c = jnp.einsum('bhqn,bnd->bqhd', p_c.astype(vc.dtype), vc)
    c_start = jnp.arange(NC)[:, None] * CMP_STRIDE
    b_start = jnp.arange(NSB)[None, :] * SEL_BLOCK
    cover = ((c_start < b_start + SEL_BLOCK) & (c_start + CMP_LEN > b_start)).astype(f32)
    imp = jnp.einsum('bhqn,nm->bqm', p_c, cover)
    blk = jnp.arange(NSB)[None, :]
    q_blk = (q_pos // SEL_BLOCK)[:, None]
    visible = blk <= q_blk
    forced = visible & ((blk == 0) | (blk > q_blk - N_LOCAL_SEL))
    score = jnp.where(forced, jnp.inf, jnp.where(visible, imp, -jnp.inf))
    top_val, idx = lax.top_k(score, min(N_SEL, NSB))
    valid = top_val > -jnp.inf
    bi = jnp.arange(B)[:, None, None]
    ks = sel_kb[bi, idx]
    vs = sel_vb[bi, idx]
    pos_s = idx[..., None] * SEL_BLOCK + jnp.arange(SEL_BLOCK)
    dist_s = q_pos[None, :, None, None] - pos_s
    mask_s = (valid[..., None] & (dist_s >= 0))[:, None]
    s_s = (jnp.einsum('bqhd,bqtkd->bhqtk', q, ks).astype(f32) * scale
           - slopes[None, :, None, None, None] * dist_s[:, None].astype(f32))
    p_s = masked_softmax(s_s, mask_s, axis=(-2, -1))
    o_s = jnp.einsum('bhqtk,bqtkd->bqhd', p_s.astype(vs.dtype), vs)
    dist_w = q_pos[:, None] - win_pos[None, :]
    mask_w = (dist_w >= 0) & (dist_w < WINDOW) & (win_pos >= 0)[None, :]
    s_w = jnp.einsum('bqhd,bkd->bhqk', q, win_k).astype(f32) * scale - slopes[:, None, None] * dist_w.astype(f32)
    p_w = masked_softmax(s_w, mask_w)
    o_w = jnp.einsum('bhqk,bkd->bqhd', p_w.astype(win_v.dtype), win_v)
    g = jax.nn.sigmoid(gates.astype(f32))
    return g[..., 0:1] * o_c + g[..., 1:2] * o_s + g[..., 2:3] * o_w


def moba_attention(q, q_pos, kb, vb, kmean, slopes):
    B, Q, H, D = q.shape
    NB = kb.shape[2]
    f32 = jnp.float32
    q_blk = q_pos // MOBA_BLOCK
    past = jnp.arange(NB)[None, :] < q_blk[:, None]
    gate = jnp.einsum('bqhd,bhnd->bhqn', q.astype(f32), kmean)
    gate = jnp.where(past, gate, -jnp.inf)
    own = jnp.broadcast_to(q_blk[None, None, :, None], (B, H, Q, 1))
    n_top = min(MOBA_TOPK, NB - 1)
    if n_top > 0:
        top_val, top_idx = lax.top_k(gate, n_top)
        idx = jnp.concatenate([top_idx.astype(own.dtype), own], axis=-1)
        valid = jnp.concatenate([top_val > -jnp.inf, jnp.ones((B, H, Q, 1), bool)], axis=-1)
    else:
        idx = own
        valid = jnp.ones((B, H, Q, 1), bool)
    bi = jnp.arange(B)[:, None, None, None]
    hi = jnp.arange(H)[None, :, None, None]
    ks = kb[bi, hi, idx]
    vs = vb[bi, hi, idx]
    pos = idx[..., None] * MOBA_BLOCK + jnp.arange(MOBA_BLOCK)
    dist = q_pos[None, None, :, None, None] - pos
    mask = valid[..., None] & (dist >= 0)
    s = (jnp.einsum('bqhd,bhqjkd->bhqjk', q, ks).astype(f32) * (HEAD_DIM ** -0.5)
         - slopes[None, :, None, None, None] * dist.astype(f32))
    p = masked_softmax(s, mask, axis=(-2, -1))
    return jnp.einsum('bhqjk,bhqjkd->bqhd', p.astype(vs.dtype), vs)


def mix_queries(q, g_nsa, cum_q, q_pos, keys, win_k, win_v, win_pos):
    k, v, cum_k, kc, vc, sel_kb, sel_vb, moba_kb, moba_vb, moba_kmean = keys
    B, Q = q.shape[:2]
    H = HEADS_PER_MIXER
    k_pos = jnp.arange(k.shape[1])
    slopes = alibi_slopes(2 * H)
    o_sb = stick_breaking_attention(head_group(q, SB_Q0, H), head_group(k, SB_KV0, H),
                                    head_group(v, SB_KV0, H), q_pos, k_pos)
    o_nsa = nsa_attention(head_group(q, NSA_Q0, H), g_nsa, q_pos, kc, vc, sel_kb, sel_vb,
                          win_k, win_v, win_pos, slopes[0::2])
    o_moba = moba_attention(head_group(q, MOBA_Q0, H), q_pos, moba_kb, moba_vb, moba_kmean, slopes[1::2])
    o_fox = forgetting_attention(head_group(q, FOX_Q0, H), head_group(k, FOX_KV0, H),
                                 head_group(v, FOX_KV0, H), cum_q, cum_k, q_pos, k_pos)
    outs = [o.reshape(B, Q, BRANCH_WIDTH).astype(q.dtype) for o in (o_sb, o_nsa, o_moba, o_fox)]
    return jnp.stack(outs, axis=2)


def mixer_prompt(h, w_in, b_fgt, cmp):
    B, S, _ = h.shape
    q, k, v, kw, vw, g, logf = project_inputs(h, w_in, b_fgt)
    cum = jnp.cumsum(logf, axis=1)
    keys = prepare_keys(k, v, cum, cmp)
    pad = ((0, 0), (WINDOW, 0), (0, 0))
    kw_pad = jnp.pad(kw[:, :, 0], pad)
    vw_pad = jnp.pad(vw[:, :, 0], pad)
    n_blk = S // Q_BLOCK

    def to_blocks(a):
        return jnp.swapaxes(a.reshape(B, n_blk, Q_BLOCK, *a.shape[2:]), 0, 1)

    def one_block(args):
        i, qb, gb, cb = args
        q0 = i * Q_BLOCK
        q_pos = q0 + jnp.arange(Q_BLOCK)
        wk = lax.dynamic_slice_in_dim(kw_pad, q0, WINDOW + Q_BLOCK, axis=1)
        wv = lax.dynamic_slice_in_dim(vw_pad, q0, WINDOW + Q_BLOCK, axis=1)
        w_pos = q0 - WINDOW + jnp.arange(WINDOW + Q_BLOCK)
        return mix_queries(qb, gb, cb, q_pos, keys, wk, wv, w_pos)

    o = lax.map(one_block, (jnp.arange(n_blk), to_blocks(q), to_blocks(g), to_blocks(cum)))
    o = jnp.swapaxes(o, 0, 1).reshape(B, S, N_BRANCH, BRANCH_WIDTH)
    wb = min(WINDOW, S)
    return o, (k, v, logf, kw[:, S - wb:], vw[:, S - wb:])


def mixer_sample(h, ck, cv, clogf, wk_buf, wv_buf, page_table, w_in, b_fgt, cmp):
    B, T, _ = h.shape
    q, k_new, v_new, kw, vw, g, logf_new = project_inputs(h, w_in, b_fgt)
    past_len = page_table.shape[1] * ck.shape[1]
    k = jnp.concatenate([gather_pages(ck, page_table), k_new], axis=1)
    v = jnp.concatenate([gather_pages(cv, page_table), v_new], axis=1)
    logf = jnp.concatenate([gather_pages(clogf, page_table).astype(jnp.float32), logf_new], axis=1)
    cum = jnp.cumsum(logf, axis=1)
    keys = prepare_keys(k, v, cum, cmp)
    wb = wk_buf.shape[1]
    wk = jnp.concatenate([wk_buf, kw], axis=1)
    wv = jnp.concatenate([wv_buf, vw], axis=1)
    q_pos = past_len + jnp.arange(T)
    w_pos = past_len - wb + jnp.arange(wb + T)
    o = mix_queries(q, g, cum[:, past_len:], q_pos, keys, wk[:, :, 0], wv[:, :, 0], w_pos)
    return o, (k_new, v_new, logf_new, wk[:, T:], wv[:, T:])


def merge_branches(h, o_br, w_branch, w_merge_gate, w_out):
    B, L, D = h.shape
    up = jnp.einsum('blnc,ncd->blnd', o_br, w_branch)
    gate = jax.nn.sigmoid(h @ w_merge_gate).reshape(B, L, N_BRANCH, D)
    return jnp.einsum('blnd,blnd->bld', gate, up) @ w_out


def setup_inputs(seed: int = 0) -> dict:
    key = jax.random.key(seed)
    keys = list(jax.random.split(key, 40))

    def nrm(shape, scale=1.0):
        return jax.random.normal(keys.pop(), shape, jnp.float32) * scale

    H = HEADS_PER_MIXER
    n_pages = PAST_LEN // PAGE_SIZE
    n_used = DEC_BATCH * n_pages
    n_phys = (5 * n_used + 3) // 4
    page_table = jax.random.permutation(keys.pop(), n_phys)[:n_used].reshape(DEC_BATCH, n_pages).astype(jnp.int32)
    wb = min(WINDOW, PAST_LEN)
    d, f = D_MODEL, D_FF
    cw = CMP_LEN * HEAD_DIM
    return {
        'x_prompt': nrm((BATCH, SEQ, d)),
        'x_sample': nrm((DEC_BATCH, DEC_SEQ, d)),
        'cache_k': nrm((DEPTH, n_phys, PAGE_SIZE, N_KV_HEADS, HEAD_DIM)),
        'cache_v': nrm((DEPTH, n_phys, PAGE_SIZE, N_KV_HEADS, HEAD_DIM)),
        'cache_logf': jax.nn.log_sigmoid(nrm((DEPTH, n_phys, PAGE_SIZE, H)) + 2.5),
        'state_win_k': nrm((DEPTH, DEC_BATCH, wb, 1, HEAD_DIM)),
        'state_win_v': nrm((DEPTH, DEC_BATCH, wb, 1, HEAD_DIM)),
        'page_table': page_table,
        'g_ffn1': 1.0 + nrm((DEPTH, d), 0.05),
        'w_ffn1_gate': nrm((DEPTH, d, f), d ** -0.5),
        'w_ffn1_up': nrm((DEPTH, d, f), d ** -0.5),
        'w_ffn1_down': nrm((DEPTH, f, d), f ** -0.5),
        'g_mix': 1.0 + nrm((DEPTH, d), 0.05),
        'w_in': nrm((DEPTH, d, IN_COLS), d ** -0.5),
        'b_fgt': jax.random.uniform(keys.pop(), (DEPTH, H), jnp.float32, 1.0, 4.0),
        'cmp_pos_k': nrm((DEPTH, CMP_LEN, HEAD_DIM), 0.1),
        'cmp_w1_k': nrm((DEPTH, cw, CMP_HIDDEN), cw ** -0.5),
        'cmp_w2_k': nrm((DEPTH, CMP_HIDDEN, HEAD_DIM), CMP_HIDDEN ** -0.5),
        'cmp_pos_v': nrm((DEPTH, CMP_LEN, HEAD_DIM), 0.1),
        'cmp_w1_v': nrm((DEPTH, cw, CMP_HIDDEN), cw ** -0.5),
        'cmp_w2_v': nrm((DEPTH, CMP_HIDDEN, HEAD_DIM), CMP_HIDDEN ** -0.5),
        'w_branch': nrm((DEPTH, N_BRANCH, BRANCH_WIDTH, d), BRANCH_WIDTH ** -0.5),
        'w_merge_gate': nrm((DEPTH, d, N_BRANCH * d), d ** -0.5),
        'w_out': nrm((DEPTH, d, d), d ** -0.5),
        'g_ffn2': 1.0 + nrm((DEPTH, d), 0.05),
        'w_ffn2_gate': nrm((DEPTH, d, f), d ** -0.5),
        'w_ffn2_up': nrm((DEPTH, d, f), d ** -0.5),
        'w_ffn2_down': nrm((DEPTH, f, d), f ** -0.5),
        'g_final': 1.0 + nrm((d,), 0.05),
    }


def reference(x_prompt, x_sample, cache_k, cache_v, cache_logf, state_win_k, state_win_v, page_table,
              g_ffn1, w_ffn1_gate, w_ffn1_up, w_ffn1_down, g_mix, w_in, b_fgt,
              cmp_pos_k, cmp_w1_k, cmp_w2_k, cmp_pos_v, cmp_w1_v, cmp_w2_v,
              w_branch, w_merge_gate, w_out, g_ffn2, w_ffn2_gate, w_ffn2_up, w_ffn2_down, g_final):
    xp, xs = x_prompt, x_sample
    pk, pv, plf, pwk, pwv = [], [], [], [], []
    sk, sv, slf, swk, swv = [], [], [], [], []
    for l in range(DEPTH):
        xp = xp + swiglu_half(xp, g_ffn1[l], w_ffn1_gate[l], w_ffn1_up[l], w_ffn1_down[l])
        xs = xs + swiglu_half(xs, g_ffn1[l], w_ffn1_gate[l], w_ffn1_up[l], w_ffn1_down[l])
        cmp = (cmp_pos_k[l], cmp_w1_k[l], cmp_w2_k[l], cmp_pos_v[l], cmp_w1_v[l], cmp_w2_v[l])
        hp = rms_norm(xp, g_mix[l])
        o_p, st_p = mixer_prompt(hp, w_in[l], b_fgt[l], cmp)
        xp = xp + merge_branches(hp, o_p, w_branch[l], w_merge_gate[l], w_out[l])
        hs = rms_norm(xs, g_mix[l])
        o_s, st_s = mixer_sample(hs, cache_k[l], cache_v[l], cache_logf[l], state_win_k[l], state_win_v[l],
                                 page_table, w_in[l], b_fgt[l], cmp)
        xs = xs + merge_branches(hs, o_s, w_branch[l], w_merge_gate[l], w_out[l])
        xp = xp + swiglu_half(xp, g_ffn2[l], w_ffn2_gate[l], w_ffn2_up[l], w_ffn2_down[l])
        xs = xs + swiglu_half(xs, g_ffn2[l], w_ffn2_gate[l], w_ffn2_up[l], w_ffn2_down[l])
        pk.append(st_p[0]); pv.append(st_p[1]); plf.append(st_p[2]); pwk.append(st_p[3]); pwv.append(st_p[4])
        sk.append(st_s[0]); sv.append(st_s[1]); slf.append(st_s[2]); swk.append(st_s[3]); swv.append(st_s[4])
    y_prompt = rms_norm(xp, g_final)
    y_sample = rms_norm(xs, g_final)
    return (y_prompt, y_sample,
            jnp.stack(pk), jnp.stack(pv), jnp.stack(plf), jnp.stack(pwk), jnp.stack(pwv),
            jnp.stack(sk), jnp.stack(sv), jnp.stack(slf), jnp.stack(swk), jnp.stack(swv))
```

```python
import functools

import numpy as np
import jax
import jax.numpy as jnp
from jax import lax
from jax.experimental import pallas as pl
from jax.experimental.pallas import tpu as pltpu

F32 = jnp.float32
BF16 = jnp.bfloat16

D_MODEL = 1024
HEAD_DIM = 64
HEADS = 4
BRANCH_WIDTH = HEADS * HEAD_DIM
N_KV_HEADS = 14
KV_COLS = N_KV_HEADS * HEAD_DIM
CMP_KV = 4
SEL_KV = 5
MOBA_KV0 = 6
FOX_KV0 = 10
CMP_STRIDE = 16
CMP_LEN = 32
CMP_HIDDEN = 128
SEL_BLOCK = 64
N_SEL = 16
N_LOCAL_SEL = 2
WINDOW = 512
MOBA_BLOCK = 256
MOBA_TOPK = 3
D_FF = 2816
RMS_EPS = 1e-6
Q_SCALE = HEAD_DIM ** -0.5
NEG = -1e30
MISC_COLS = 256
GATE_LANE0 = 0
FGT_LANE0 = 12
NSA_SLOPES = (2.0 ** -1, 2.0 ** -3, 2.0 ** -5, 2.0 ** -7)
MOBA_SLOPES = (2.0 ** -2, 2.0 ** -4, 2.0 ** -6, 2.0 ** -8)

VMEM_LIMIT = 56 * 1024 * 1024


def _cparams(sem):
    return pltpu.CompilerParams(dimension_semantics=sem, vmem_limit_bytes=VMEM_LIMIT)


def _dot(a, b):
    return jnp.dot(a, b, preferred_element_type=F32)


def _rms(x, g):
    return x * lax.rsqrt(jnp.mean(x * x, axis=-1, keepdims=True) + RMS_EPS) * g


def _softplus(z):
    return jnp.maximum(z, 0.0) + jnp.log(1.0 + jnp.exp(-jnp.abs(z)))


def _sigmoid(z):
    return 1.0 / (1.0 + jnp.exp(-z))


def _split2(a):
    hi = a.astype(BF16)
    lo = (a - hi.astype(F32)).astype(BF16)
    return hi, lo


def _split3(a):
    a1 = a.astype(BF16)
    r1 = a - a1.astype(F32)
    a2 = r1.astype(BF16)
    a3 = (r1 - a2.astype(F32)).astype(BF16)
    return a1, a2, a3


def _ffn_body(x_ref, g_ref, wg_ref, wu_ref, wd_ref, gf_ref, o_ref, h_scr, acc_scr, *, n_f, final_norm):
    j = pl.program_id(1)

    @pl.when(j == 0)
    def _():
        h_scr[...] = _rms(x_ref[...], g_ref[...]).astype(BF16)
        acc_scr[...] = jnp.zeros_like(acc_scr)

    h = h_scr[...]
    a = _dot(h, wg_ref[...])
    u = _dot(h, wu_ref[...])
    act = (a * _sigmoid(a)) * u
    acc_scr[...] += _dot(act.astype(BF16), wd_ref[...])

    @pl.when(j == n_f - 1)
    def _():
        y = x_ref[...] + 0.5 * acc_scr[...]
        if final_norm:
            y = _rms(y, gf_ref[...])
        o_ref[...] = y


def ffn_half(x, g, wg, wu, wd, g_final=None, *, tm, tf=1408):
    m, d = x.shape
    f = wg.shape[1]
    n_f = f // tf
    final_norm = g_final is not None
    gf = g_final if final_norm else g
    body = functools.partial(_ffn_body, n_f=n_f, final_norm=final_norm)
    return pl.pallas_call(
        body,
        grid=(m // tm, n_f),
        in_specs=[
            pl.BlockSpec((tm, d), lambda i, j: (i, 0)),
            pl.BlockSpec((1, d), lambda i, j: (0, 0)),
            pl.BlockSpec((d, tf), lambda i, j: (0, j)),
            pl.BlockSpec((d, tf), lambda i, j: (0, j)),
            pl.BlockSpec((tf, d), lambda i, j: (j, 0)),
            pl.BlockSpec((1, d), lambda i, j: (0, 0)),
        ],
        out_specs=pl.BlockSpec((tm, d), lambda i, j: (i, 0)),
        out_shape=jax.ShapeDtypeStruct((m, d), F32),
        scratch_shapes=[pltpu.VMEM((tm, d), BF16), pltpu.VMEM((tm, d), F32)],
        compiler_params=_cparams(("parallel", "arbitrary")),
        name="ffn_half",
    )(x, g.reshape(1, d), wg, wu, wd, gf.reshape(1, d))


def _proj_body(x_ref, g_ref, wq_ref, wk_ref, wv_ref, wm_ref, bias_ref, tri_ref,
               h_ref, q_ref, k_ref, v_ref, kb_ref, vb_ref, misc_ref, *rest, tm, tiles_per_seq, prompt):
    x = x_ref[...]
    h = _rms(x, g_ref[...]).astype(BF16)
    h_ref[...] = h
    q_ref[...] = (_dot(h, wq_ref[...]) * Q_SCALE).astype(BF16)
    k = _dot(h, wk_ref[...])
    k_ref[...] = k
    kb_ref[...] = k.astype(BF16)
    v = _dot(h, wv_ref[...])
    v_ref[...] = v
    vb_ref[...] = v.astype(BF16)
    mz = _dot(h, wm_ref[...])
    grp = mz[:, 128:256]
    lane = lax.broadcasted_iota(jnp.int32, grp.shape, 1)
    is_f = (lane >= FGT_LANE0) & (lane < FGT_LANE0 + HEADS)
    xf = grp + bias_ref[...]
    lf = jnp.minimum(xf, 0.0) - jnp.log(1.0 + jnp.exp(-jnp.abs(xf)))
    lf = jnp.where(is_f, lf, 0.0)
    misc_ref[:, 0:128] = mz[:, 0:128]
    misc_ref[:, 128:256] = jnp.where(is_f, lf, grp)
    if prompt:
        cum_ref, kmean_ref, carry_scr = rest
        i = pl.program_id(0)

        @pl.when(i % tiles_per_seq == 0)
        def _():
            carry_scr[...] = jnp.zeros_like(carry_scr)

        tri = tri_ref[...]
        a1, a2, a3 = _split3(lf)
        cs = _dot(tri, a1) + _dot(tri, a2) + _dot(tri, a3) + carry_scr[0:1, :]
        cum_ref[...] = cs
        carry_scr[0:1, :] = cs[tm - 1:tm, :]
        for r in range(tm // MOBA_BLOCK):
            kblk = k[r * MOBA_BLOCK:(r + 1) * MOBA_BLOCK, MOBA_KV0 * HEAD_DIM:(MOBA_KV0 + HEADS) * HEAD_DIM]
            kmean_ref[r] = jnp.sum(kblk, axis=0, keepdims=True) * (1.0 / MOBA_BLOCK)


def mixer_project(x, g, wq, wk, wv, wm, bias, *, tm, seq, prompt):
    m, d = x.shape
    tiles_per_seq = seq // tm if prompt else 1
    tri = jnp.asarray(np.tril(np.ones((tm, tm), np.float32)), BF16)
    row = lambda i: (i, 0)
    const = lambda i: (0, 0)
    out_shape = [
        jax.ShapeDtypeStruct((m, d), BF16),
        jax.ShapeDtypeStruct((m, d), BF16),
        jax.ShapeDtypeStruct((m, KV_COLS), F32),
        jax.ShapeDtypeStruct((m, KV_COLS), F32),
        jax.ShapeDtypeStruct((m, KV_COLS), BF16),
        jax.ShapeDtypeStruct((m, KV_COLS), BF16),
        jax.ShapeDtypeStruct((m, MISC_COLS), F32),
    ]
    out_specs = [
        pl.BlockSpec((tm, d), row), pl.BlockSpec((tm, d), row),
        pl.BlockSpec((tm, KV_COLS), row), pl.BlockSpec((tm, KV_COLS), row),
        pl.BlockSpec((tm, KV_COLS), row), pl.BlockSpec((tm, KV_COLS), row),
        pl.BlockSpec((tm, MISC_COLS), row),
    ]
    scratch = []
    if prompt:
        nb = tm // MOBA_BLOCK
        out_shape += [jax.ShapeDtypeStruct((m, 128), F32),
                      jax.ShapeDtypeStruct((m // MOBA_BLOCK, 1, BRANCH_WIDTH), F32)]
        out_specs += [pl.BlockSpec((tm, 128), row),
                      pl.BlockSpec((nb, 1, BRANCH_WIDTH), lambda i: (i, 0, 0))]
        scratch = [pltpu.VMEM((8, 128), F32)]
    body = functools.partial(_proj_body, tm=tm, tiles_per_seq=tiles_per_seq, prompt=prompt)
    return pl.pallas_call(
        body,
        grid=(m // tm,),
        in_specs=[
            pl.BlockSpec((tm, d), row),
            pl.BlockSpec((1, d), const),
            pl.BlockSpec((d, d), const),
            pl.BlockSpec((d, KV_COLS), const),
            pl.BlockSpec((d, KV_COLS), const),
            pl.BlockSpec((d, MISC_COLS), const),
            pl.BlockSpec((1, 128), const),
            pl.BlockSpec((tm, tm), const),
        ],
        out_specs=out_specs,
        out_shape=out_shape,
        scratch_shapes=scratch,
        compiler_params=_cparams(("arbitrary",)),
        name="mixer_project",
    )(x, g.reshape(1, d), wq, wk, wv, wm, bias, tri)


def _gelu_tanh(x):
    return 0.5 * x * (1.0 + jnp.tanh(np.sqrt(2.0 / np.pi).astype(np.float32) * (x + 0.044715 * (x * x * x))))


def _compress_body(xs_ref, pos_ref, w1a_ref, w1b_ref, w2_ref, o_ref):
    xs = xs_ref[0, 0]
    n16 = xs.shape[0]
    pos = pos_ref[0]
    a = _dot((xs + pos[0:1, :]).astype(BF16), w1a_ref[0])
    b = _dot((xs + pos[1:2, :]).astype(BF16), w1b_ref[0])
    hid = a + pltpu.roll(b, n16 - 1, 0)
    o_ref[0, 0] = _dot(_gelu_tanh(hid).astype(BF16), w2_ref[0])


def nsa_compress(xs, pos, w1, w2):
    _, b, n16, cw = xs.shape
    pos2 = pos.reshape(2, 2, cw)
    w1a = w1[:, :cw].astype(BF16)
    w1b = w1[:, cw:].astype(BF16)
    return pl.pallas_call(
        _compress_body,
        grid=(2, b),
        in_specs=[
            pl.BlockSpec((1, 1, n16, cw), lambda t, i: (t, i, 0, 0)),
            pl.BlockSpec((1, 2, cw), lambda t, i: (t, 0, 0)),
            pl.BlockSpec((1, cw, CMP_HIDDEN), lambda t, i: (t, 0, 0)),
            pl.BlockSpec((1, cw, CMP_HIDDEN), lambda t, i: (t, 0, 0)),
            pl.BlockSpec((1, CMP_HIDDEN, HEAD_DIM), lambda t, i: (t, 0, 0)),
        ],
        out_specs=pl.BlockSpec((1, 1, n16, HEAD_DIM), lambda t, i: (t, i, 0, 0)),
        out_shape=jax.ShapeDtypeStruct((2, b, n16, HEAD_DIM), F32),
        compiler_params=_cparams(("parallel", "parallel")),
        name="nsa_compress",
    )(xs, pos2, w1a, w1b, w2.astype(BF16))


TQ = 256
TK = 256


def _pair_lanes(h):
    p = h // 2
    return p * 128, (p + 1) * 128


def _merge_pairs(accs):
    lane = lax.broadcasted_iota(jnp.int32, accs[0].shape, 1)
    lo = lane < HEAD_DIM
    return jnp.concatenate([jnp.where(lo, accs[0], accs[1]), jnp.where(lo, accs[2], accs[3])], axis=1)


def _online(s, mask, m, l, acc, vp):
    m_new = jnp.maximum(m, jnp.max(s, axis=1, keepdims=True))
    alpha = jnp.exp(m - m_new)
    p = jnp.exp(s - m_new)
    if mask is not None:
        p = jnp.where(mask, p, 0.0)
    l = alpha * l + jnp.sum(p, axis=1, keepdims=True)
    acc = alpha * acc + _dot(p.astype(BF16), vp)
    return m_new, l, acc


def _finish(l, acc):
    return acc / jnp.maximum(l, 1e-30)


def _sb_body(q_ref, kT_ref, v_ref, u_ref, o_ref):
    qi = pl.program_id(1)
    row = qi * TQ + lax.broadcasted_iota(jnp.int32, (TQ, TK), 0)
    lcol = lax.broadcasted_iota(jnp.int32, (TQ, TK), 1)
    u = u_ref[...]
    accs = []
    for h in range(HEADS):
        qh = q_ref[0, :, h * HEAD_DIM:(h + 1) * HEAD_DIM]
        l0, l1 = _pair_lanes(h)

        def body(jj, carry, qh=qh, h=h, l0=l0, l1=l1):
            acc, run = carry
            k0 = pl.multiple_of((qi - jj) * TK, TK)
            z = _dot(qh, kT_ref[0, h * HEAD_DIM:(h + 1) * HEAD_DIM, pl.ds(k0, TK)])
            before = (k0 + lcol) < row
            sp = _softplus(z)
            lk = jnp.where(before, -sp, 0.0)
            hi, lo = _split2(lk)
            loc = _dot(hi, u) + _dot(lo, u)
            w = jnp.where(before, jnp.exp(z - sp + loc + run), 0.0)
            acc = acc + _dot(w.astype(BF16), v_ref[0, pl.ds(k0, TK), l0:l1])
            run = run + (loc[:, 0:1] + lk[:, 0:1])
            return acc, run

        acc, _ = lax.fori_loop(0, qi + 1, body, (jnp.zeros((TQ, 128), F32), jnp.zeros((TQ, 1), F32)))
        accs.append(acc)
    o_ref[0] = _merge_pairs(accs).astype(BF16)


def _strict_upper_sum_matrix(n):
    return jnp.asarray(np.tril(np.ones((n, n), np.float32), -1), BF16)


def sb_attention(q, kT, v):
    b, s, _ = q.shape
    return pl.pallas_call(
        _sb_body,
        grid=(b, s // TQ),
        in_specs=[
            pl.BlockSpec((1, TQ, BRANCH_WIDTH), lambda i, j: (i, j, 0)),
            pl.BlockSpec((1, BRANCH_WIDTH, s), lambda i, j: (i, 0, 0)),
            pl.BlockSpec((1, s, BRANCH_WIDTH), lambda i, j: (i, 0, 0)),
            pl.BlockSpec((TK, TK), lambda i, j: (0, 0)),
        ],
        out_specs=pl.BlockSpec((1, TQ, BRANCH_WIDTH), lambda i, j: (i, j, 0)),
        out_shape=jax.ShapeDtypeStruct((b, s, BRANCH_WIDTH), BF16),
        compiler_params=_cparams(("parallel", "arbitrary")),
        name="sb_attention",
    )(q, kT, v, _strict_upper_sum_matrix(TK))


def _fox_body(q_ref, kT_ref, v_ref, cq_ref, ck_ref, o_ref):
    qi = pl.program_id(1)
    row = qi * TQ + lax.broadcasted_iota(jnp.int32, (TQ, TK), 0)
    lcol = lax.broadcasted_iota(jnp.int32, (TQ, TK), 1)
    accs = []
    for h in range(HEADS):
        qh = q_ref[0, :, h * HEAD_DIM:(h + 1) * HEAD_DIM]
        cq = cq_ref[0, :, FGT_LANE0 + h:FGT_LANE0 + h + 1]
        l0, l1 = _pair_lanes(h)

        def body(j, carry, qh=qh, cq=cq, h=h, l0=l0, l1=l1):
            m, l, acc = carry
            k0 = pl.multiple_of(j * TK, TK)
            z = _dot(qh, kT_ref[0, h * HEAD_DIM:(h + 1) * HEAD_DIM, pl.ds(k0, TK)])
            s = z + cq - ck_ref[0, h:h + 1, pl.ds(k0, TK)]
            mask = (k0 + lcol) <= row
            s = jnp.where(mask, s, NEG)
            return _online(s, mask, m, l, acc, v_ref[0, pl.ds(k0, TK), l0:l1])

        init = (jnp.full((TQ, 1), NEG, F32), jnp.zeros((TQ, 1), F32), jnp.zeros((TQ, 128), F32))
        _, l, acc = lax.fori_loop(0, qi + 1, body, init)
        accs.append(_finish(l, acc))
    o_ref[0] = _merge_pairs(accs).astype(BF16)


def fox_attention(q, kT, v, cum, cumT):
    b, s, _ = q.shape
    return pl.pallas_call(
        _fox_body,
        grid=(b, s // TQ),
        in_specs=[
            pl.BlockSpec((1, TQ, BRANCH_WIDTH), lambda i, j: (i, j, 0)),
            pl.BlockSpec((1, BRANCH_WIDTH, s), lambda i, j: (i, 0, 0)),
            pl.BlockSpec((1, s, BRANCH_WIDTH), lambda i, j: (i, 0, 0)),
            pl.BlockSpec((1, TQ, 128), lambda i, j: (i, j, 0)),
            pl.BlockSpec((1, 8, s), lambda i, j: (i, 0, 0)),
        ],
        out_specs=pl.BlockSpec((1, TQ, BRANCH_WIDTH), lambda i, j: (i, j, 0)),
        out_shape=jax.ShapeDtypeStruct((b, s, BRANCH_WIDTH), BF16),
        compiler_params=_cparams(("parallel", "arbitrary")),
        name="fox_attention",
    )(q, kT, v, cum, cumT)


def _rank_select(score, n_cols, k):
    lane = lax.broadcasted_iota(jnp.int32, score.shape, 1)
    rank = jnp.zeros(score.shape, F32)
    for c in range(n_cols):
        col = score[:, c:c + 1]
        ahead = jnp.where(col > score, 1.0, jnp.where(col == score, jnp.where(lane > c, 1.0, 0.0), 0.0))
        rank = rank + ahead
    return jnp.where(rank < k, 1.0, 0.0)


def _moba_body(q_ref, kT_ref, v_ref, kmT_ref, em_ref, o_ref):
    qi = pl.program_id(1)
    row = qi * TQ + lax.broadcasted_iota(jnp.int32, (TQ, TK), 0)
    lcol = lax.broadcasted_iota(jnp.int32, (TQ, TK), 1)
    lane = lax.broadcasted_iota(jnp.int32, (TQ, 128), 1)
    n_blocks = kT_ref.shape[2] // MOBA_BLOCK
    accs = []
    for h in range(HEADS):
        qh = q_ref[0, :, h * HEAD_DIM:(h + 1) * HEAD_DIM]
        gate = _dot(qh, kmT_ref[0, h * HEAD_DIM:(h + 1) * HEAD_DIM, :])
        gate = jnp.where(lane < qi, gate, NEG)
        sel = _rank_select(gate, n_blocks, MOBA_TOPK)
        sel = jnp.where(lane < qi, sel, jnp.where(lane == qi, 1.0, 0.0)).astype(BF16)
        l0, l1 = _pair_lanes(h)
        slope = MOBA_SLOPES[h]

        def body(j, carry, qh=qh, sel=sel, h=h, l0=l0, l1=l1, slope=slope):
            m, l, acc = carry
            k0 = pl.multiple_of(j * TK, TK)
            z = _dot(qh, kT_ref[0, h * HEAD_DIM:(h + 1) * HEAD_DIM, pl.ds(k0, TK)])
            dist = row - (k0 + lcol)
            selm = _dot(sel, em_ref[:, pl.ds(k0, TK)])
            mask = jnp.where(dist >= 0, selm, 0.0) > 0.5
            s = jnp.where(mask, z - slope * dist.astype(F32), NEG)
            return _online(s, mask, m, l, acc, v_ref[0, pl.ds(k0, TK), l0:l1])

        init = (jnp.full((TQ, 1), NEG, F32), jnp.zeros((TQ, 1), F32), jnp.zeros((TQ, 128), F32))
        _, l, acc = lax.fori_loop(0, qi + 1, body, init)
        accs.append(_finish(l, acc))
    o_ref[0] = _merge_pairs(accs).astype(BF16)


def _block_expand_matrix(block, s):
    e = (np.arange(128)[:, None] == (np.arange(s)[None, :] // block)).astype(np.float32)
    return jnp.asarray(e, BF16)


def moba_attention(q, kT, v, kmeanT):
    b, s, _ = q.shape
    return pl.pallas_call(
        _moba_body,
        grid=(b, s // TQ),
        in_specs=[
            pl.BlockSpec((1, TQ, BRANCH_WIDTH), lambda i, j: (i, j, 0)),
            pl.BlockSpec((1, BRANCH_WIDTH, s), lambda i, j: (i, 0, 0)),
            pl.BlockSpec((1, s, BRANCH_WIDTH), lambda i, j: (i, 0, 0)),
            pl.BlockSpec((1, BRANCH_WIDTH, 128), lambda i, j: (i, 0, 0)),
            pl.BlockSpec((128, s), lambda i, j: (0, 0)),
        ],
        out_specs=pl.BlockSpec((1, TQ, BRANCH_WIDTH), lambda i, j: (i, j, 0)),
        out_shape=jax.ShapeDtypeStruct((b, s, BRANCH_WIDTH), BF16),
        compiler_params=_cparams(("parallel", "arbitrary")),
        name="moba_attention",
    )(q, kT, v, kmeanT, _block_expand_matrix(MOBA_BLOCK, s))


def _nsa_body(q_ref, kT_ref, v_ref, kcT_ref, vc_ref, kwT_ref, vw_ref, g_ref, cover_ref, es_ref, o_ref, *, n_cmp):
    qi = pl.program_id(1)
    s_len = kT_ref.shape[2]
    n_sel_blocks = s_len // SEL_BLOCK
    row = qi * TQ + lax.broadcasted_iota(jnp.int32, (TQ, TK), 0)
    lcol = lax.broadcasted_iota(jnp.int32, (TQ, TK), 1)
    qs = [q_ref[0, :, h * HEAD_DIM:(h + 1) * HEAD_DIM] for h in range(HEADS)]

    ncp = kcT_ref.shape[2]
    crow = qi * TQ + lax.broadcasted_iota(jnp.int32, (TQ, ncp), 0)
    cn = lax.broadcasted_iota(jnp.int32, (TQ, ncp), 1)
    dist_c = crow - (cn * CMP_STRIDE + (CMP_LEN - 1))
    mask_c = jnp.where(cn < n_cmp, dist_c, -1) >= 0
    dist_cf = dist_c.astype(F32)
    kcT = kcT_ref[0]
    vc = vc_ref[0]
    cover = cover_ref[...]
    o_c = []
    imp = jnp.zeros((TQ, 128), F32)
    for h in range(HEADS):
        s = jnp.where(mask_c, _dot(qs[h], kcT) - NSA_SLOPES[h] * dist_cf, NEG)
        m = jnp.max(s, axis=1, keepdims=True)
        p = jnp.where(mask_c, jnp.exp(s - m), 0.0)
        p = p / jnp.maximum(jnp.sum(p, axis=1, keepdims=True), 1e-30)
        pb = p.astype(BF16)
        o_c.append(_dot(pb, vc))
        imp = imp + _dot(pb, cover)

    lane = lax.broadcasted_iota(jnp.int32, (TQ, 128), 1)
    q_blk = (qi * TQ + lax.broadcasted_iota(jnp.int32, (TQ, 128), 0)) // SEL_BLOCK
    forced = (lane == 0) | (lane > q_blk - N_LOCAL_SEL)
    score = jnp.where(lane <= q_blk, jnp.where(forced, -NEG, imp), NEG)
    sel = _rank_select(score, n_sel_blocks, N_SEL)
    sel = jnp.where(lane <= q_blk, sel, 0.0).astype(BF16)

    def sel_body(j, carry):
        k0 = pl.multiple_of(j * TK, TK)
        kT = kT_ref[0, :, pl.ds(k0, TK)]
        vp = v_ref[0, pl.ds(k0, TK), :]
        dist = row - (k0 + lcol)
        selm = _dot(sel, es_ref[:, pl.ds(k0, TK)])
        mask = jnp.where(dist >= 0, selm, 0.0) > 0.5
        distf = dist.astype(F32)
        out = []
        for h in range(HEADS):
            m, l, acc = carry[h]
            s = jnp.where(mask, _dot(qs[h], kT) - NSA_SLOPES[h] * distf, NEG)
            out.append(_online(s, mask, m, l, acc, vp))
        return tuple(out)

    init = tuple((jnp.full((TQ, 1), NEG, F32), jnp.zeros((TQ, 1), F32), jnp.zeros((TQ, 128), F32))
                 for _ in range(HEADS))
    res = lax.fori_loop(0, qi + 1, sel_body, init)
    o_s = [_finish(l, acc) for (_, l, acc) in res]

    def win_body(j, carry):
        k0 = pl.multiple_of(j * TK, TK)
        kT = kwT_ref[0, :, pl.ds(k0, TK)]
        vp = vw_ref[0, pl.ds(k0, TK), :]
        dist = row - (k0 + lcol)
        mask = jnp.where(dist >= 0, dist, WINDOW) < WINDOW
        distf = dist.astype(F32)
        out = []
        for h in range(HEADS):
            m, l, acc = carry[h]
            s = jnp.where(mask, _dot(qs[h], kT) - NSA_SLOPES[h] * distf, NEG)
            out.append(_online(s, mask, m, l, acc, vp))
        return tuple(out)

    init_w = tuple((jnp.full((TQ, 1), NEG, F32), jnp.zeros((TQ, 1), F32), jnp.zeros((TQ, HEAD_DIM), F32))
                   for _ in range(HEADS))
    res_w = lax.fori_loop(jnp.maximum(qi - WINDOW // TK, 0), qi + 1, win_body, init_w)
    o_w = [_finish(l, acc) for (_, l, acc) in res_w]

    g = _sigmoid(g_ref[0])
    outs = []
    for h in range(HEADS):
        g0 = g[:, 3 * h + 0:3 * h + 1]
        g1 = g[:, 3 * h + 1:3 * h + 2]
        g2 = g[:, 3 * h + 2:3 * h + 3]
        outs.append(g0 * o_c[h] + g1 * o_s[h][:, HEAD_DIM:2 * HEAD_DIM] + g2 * o_w[h])
    o_ref[0] = jnp.concatenate(outs, axis=1).astype(BF16)


def _cover_matrix():
    n = np.arange(256)[:, None] * CMP_STRIDE
    m = np.arange(128)[None, :] * SEL_BLOCK
    return jnp.asarray(((n < m + SEL_BLOCK) & (n + CMP_LEN > m)).astype(np.float32), BF16)


def nsa_attention(q, kselT, vpair, kcT, vc, kwT, vw, misc, n_cmp):
    b, s, _ = q.shape
    ncp = kcT.shape[2]
    body = functools.partial(_nsa_body, n_cmp=n_cmp)
    return pl.pallas_call(
        body,
        grid=(b, s // TQ),
        in_specs=[
            pl.BlockSpec((1, TQ, BRANCH_WIDTH), lambda i, j: (i, j, 0)),
            pl.BlockSpec((1, HEAD_DIM, s), lambda i, j: (i, 0, 0)),
            pl.BlockSpec((1, s, 128), lambda i, j: (i, 0, 0)),
            pl.BlockSpec((1, HEAD_DIM, ncp), lambda i, j: (i, 0, 0)),
            pl.BlockSpec((1, ncp, HEAD_DIM), lambda i, j: (i, 0, 0)),
            pl.BlockSpec((1, HEAD_DIM, s), lambda i, j: (i, 0, 0)),
            pl.BlockSpec((1, s, HEAD_DIM), lambda i, j: (i, 0, 0)),
            pl.BlockSpec((1, TQ, 128), lambda i, j: (i, j, 1)),
            pl.BlockSpec((ncp, 128), lambda i, j: (0, 0)),
            pl.BlockSpec((128, s), lambda i, j: (0, 0)),
        ],
        out_specs=pl.BlockSpec((1, TQ, BRANCH_WIDTH), lambda i, j: (i, j, 0)),
        out_shape=jax.ShapeDtypeStruct((b, s, BRANCH_WIDTH), BF16),
        compiler_params=_cparams(("parallel", "arbitrary")),
        name="nsa_attention",
    )(q, kselT, vpair, kcT, vc, kwT, vw, misc, _cover_matrix()[:ncp], _block_expand_matrix(SEL_BLOCK, s))


def _merge_body(x_ref, h_ref, o0_ref, o1_ref, o2_ref, o3_ref, wb_ref, wg_ref, wo_ref, out_ref):
    h = h_ref[...]
    d = x_ref.shape[1]
    mix = None
    for n, o_ref in enumerate((o0_ref, o1_ref, o2_ref, o3_ref)):
        gate = _sigmoid(_dot(h, wg_ref[:, n * d:(n + 1) * d]))
        term = gate * _dot(o_ref[...], wb_ref[n])
        mix = term if mix is None else mix + term
    out_ref[...] = x_ref[...] + _dot(mix.astype(BF16), wo_ref[...])


def merge_branches(x, h, outs, wb, wg, wo, *, tm):
    m, d = x.shape
    row = lambda i: (i, 0)
    return pl.pallas_call(
        _merge_body,
        grid=(m // tm,),
        in_specs=[
            pl.BlockSpec((tm, d), row),
            pl.BlockSpec((tm, d), row),
            pl.BlockSpec((tm, BRANCH_WIDTH), row),
            pl.BlockSpec((tm, BRANCH_WIDTH), row),
            pl.BlockSpec((tm, BRANCH_WIDTH), row),
            pl.BlockSpec((tm, BRANCH_WIDTH), row),
            pl.BlockSpec((4, BRANCH_WIDTH, d), lambda i: (0, 0, 0)),
            pl.BlockSpec((d, 4 * d), lambda i: (0, 0)),
            pl.BlockSpec((d, d), lambda i: (0, 0)),
        ],
        out_specs=pl.BlockSpec((tm, d), row),
        out_shape=jax.ShapeDtypeStruct((m, d), F32),
        compiler_params=_cparams(("parallel",)),
        name="merge_branches",
    )(x, h, *outs, wb, wg, wo)


def _split_w_in(w_in, b_fgt):
    d = w_in.shape[0]
    wq = w_in[:, :D_MODEL].astype(BF16)
    wk = w_in[:, D_MODEL:D_MODEL + KV_COLS].astype(BF16)
    wv = w_in[:, D_MODEL + KV_COLS:D_MODEL + 2 * KV_COLS].astype(BF16)
    rest = w_in[:, D_MODEL + 2 * KV_COLS:]
    wm = jnp.pad(rest, ((0, 0), (0, MISC_COLS - rest.shape[1]))).astype(BF16)
    bias = jnp.zeros((1, 128), F32).at[0, FGT_LANE0:FGT_LANE0 + HEADS].set(b_fgt)
    return wq, wk, wv, wm, bias


def _heads_T(kb, head0, n_heads):
    return jnp.swapaxes(kb[:, :, head0 * HEAD_DIM:(head0 + n_heads) * HEAD_DIM], 1, 2)


def prompt_branches(x, g_mix, w_parts, cmp, *, batch, seq, tm):
    wq, wk, wv, wm, bias = w_parts
    h, q, k, v, kb, vb, misc, cum, kmean = mixer_project(x, g_mix, wq, wk, wv, wm, bias, tm=tm, seq=seq, prompt=True)
    B, S = batch, seq
    q3 = q.reshape(B, S, D_MODEL)
    kb3 = kb.reshape(B, S, KV_COLS)
    vb3 = vb.reshape(B, S, KV_COLS)
    misc3 = misc.reshape(B, S, MISC_COLS)
    cum3 = cum.reshape(B, S, 128)

    def qm(i):
        return q3[:, :, i * BRANCH_WIDTH:(i + 1) * BRANCH_WIDTH]

    o_sb = sb_attention(qm(0), _heads_T(kb3, 0, HEADS), vb3[:, :, 0:BRANCH_WIDTH])

    pos_k, w1k, w2k, pos_v, w1v, w2v = cmp
    n16 = S // CMP_STRIDE
    cw = CMP_STRIDE * HEAD_DIM
    k3 = k.reshape(B, S, KV_COLS)
    v3 = v.reshape(B, S, KV_COLS)
    xs = jnp.stack([k3[:, :, CMP_KV * HEAD_DIM:(CMP_KV + 1) * HEAD_DIM].reshape(B, n16, cw),
                    v3[:, :, CMP_KV * HEAD_DIM:(CMP_KV + 1) * HEAD_DIM].reshape(B, n16, cw)])
    kvc = nsa_compress(xs, jnp.stack([pos_k, pos_v]), jnp.stack([w1k, w1v]), jnp.stack([w2k, w2v]))
    kcT = jnp.swapaxes(kvc[0], 1, 2).astype(BF16)
    vc = kvc[1].astype(BF16)
    kwT = jnp.swapaxes(misc3[:, :, 0:HEAD_DIM], 1, 2).astype(BF16)
    vw = misc3[:, :, HEAD_DIM:2 * HEAD_DIM].astype(BF16)
    o_nsa = nsa_attention(qm(1), _heads_T(kb3, SEL_KV, 1), vb3[:, :, 256:384], kcT, vc, kwT, vw, misc3, n16 - 1)

    kmeanT = jnp.swapaxes(kmean.reshape(B, S // MOBA_BLOCK, BRANCH_WIDTH), 1, 2)
    kmeanT = jnp.pad(kmeanT, ((0, 0), (0, 0), (0, 128 - S // MOBA_BLOCK))).astype(BF16)
    o_moba = moba_attention(qm(2), _heads_T(kb3, MOBA_KV0, HEADS),
                            vb3[:, :, MOBA_KV0 * HEAD_DIM:(MOBA_KV0 + HEADS) * HEAD_DIM], kmeanT)

    cumT = jnp.swapaxes(cum3[:, :, FGT_LANE0:FGT_LANE0 + HEADS], 1, 2)
    cumT = jnp.pad(cumT, ((0, 0), (0, 8 - HEADS), (0, 0)))
    o_fox = fox_attention(qm(3), _heads_T(kb3, FOX_KV0, HEADS),
                          vb3[:, :, FOX_KV0 * HEAD_DIM:(FOX_KV0 + HEADS) * HEAD_DIM], cum3, cumT)

    outs = [o.reshape(B * S, BRANCH_WIDTH) for o in (o_sb, o_nsa, o_moba, o_fox)]
    wb_len = min(WINDOW, S)
    state = (k.reshape(B, S, N_KV_HEADS, HEAD_DIM), v.reshape(B, S, N_KV_HEADS, HEAD_DIM),
             misc3[:, :, 128 + FGT_LANE0:128 + FGT_LANE0 + HEADS],
             misc3[:, S - wb_len:, 0:HEAD_DIM].reshape(B, wb_len, 1, HEAD_DIM),
             misc3[:, S - wb_len:, HEAD_DIM:2 * HEAD_DIM].reshape(B, wb_len, 1, HEAD_DIM))
    return h, outs, state


def prompt_mixers(x, g_mix, w_parts, cmp, wb, wg, wo, *, batch, seq, tm):
    h, outs, state = prompt_branches(x, g_mix, w_parts, cmp, batch=batch, seq=seq, tm=tm)
    return merge_branches(x, h, outs, wb, wg, wo, tm=tm), state


PAGE = 128
N_PAGES = 16
PAST = PAGE * N_PAGES
Q_ROWS = 32
SUM_CHUNK = 256
G_SB, G_NSA, G_MOBA, G_FOX = 0, 1, 2, 3


def _dot_nt(a, b):
    return lax.dot_general(a, b, (((1,), (1,)), ((), ())), preferred_element_type=F32)


def _row_const(vals):
    r = lax.broadcasted_iota(jnp.int32, (8, 1), 0)
    out = jnp.zeros((8, 1), F32)
    for i, v in enumerate(vals):
        out = jnp.where(r == i, v, out)
    return out


def _suffix_sum_excl(x, u, n_split):
    n = x.shape[1] // SUM_CHUNK
    carry = jnp.zeros((x.shape[0], 1), F32)
    outs = [None] * n
    for c in reversed(range(n)):
        xc = x[:, c * SUM_CHUNK:(c + 1) * SUM_CHUNK]
        parts = _split3(xc) if n_split == 3 else _split2(xc)
        loc = _dot(parts[0], u)
        for p in parts[1:]:
            loc = loc + _dot(p, u)
        outs[c] = loc + carry
        carry = carry + (loc[:, 0:1] + xc[:, 0:1])
    return jnp.concatenate(outs, axis=1)


def _decode_body(pt_ref, *refs):
    kp = refs[0:N_PAGES]
    vp = refs[N_PAGES:2 * N_PAGES]
    lfp = refs[2 * N_PAGES:3 * N_PAGES]
    (qm_ref, qn_ref, kn_ref, vn_ref, hs_ref, wn_ref, wkb_ref, wvb_ref, pos_ref, w1_ref, w2_ref,
     u_ref, cover_ref, es_ref, o_ref, s_scr, p_scr, cmp_scr) = refs[3 * N_PAGES:]
    del pt_ref
    qm = qm_ref[0]
    lane8 = lax.broadcasted_iota(jnp.int32, (8, 128), 1)
    row8 = lax.broadcasted_iota(jnp.int32, (8, 128), 0)
    live = row8[:, 0:1] < HEADS

    ksum = []
    for j in range(N_PAGES):
        kpage = kp[j][0]
        s_scr[:, j * PAGE:(j + 1) * PAGE] = _dot_nt(qm, kpage.astype(BF16))
        ksum.append(jnp.sum(kpage, axis=0, keepdims=True))
        cmp_scr[0, j * PAGE:(j + 1) * PAGE, :] = kpage[:, 256:384]
        cmp_scr[1, j * PAGE:(j + 1) * PAGE, :] = vp[j][0, :, 256:384]
    kn = kn_ref[0].astype(BF16).astype(F32)
    z_new = jnp.sum(qm.astype(F32) * kn, axis=1, keepdims=True)

    pos_f = lax.broadcasted_iota(jnp.int32, (8, PAST), 1).astype(F32)
    dist_f = float(PAST) - pos_f
    u = u_ref[...]
    hs = hs_ref[0]

    z = s_scr[8 * G_SB:8 * G_SB + 8, :]
    sp = _softplus(z)
    later = _suffix_sum_excl(-sp, u, 2)
    p_scr[8 * G_SB:8 * G_SB + 8, :] = jnp.exp(z - sp + later).astype(BF16)
    pnew = [jnp.zeros((8, 1), F32)]
    inv = [jnp.ones((8, 1), F32)]

    kvc = []
    for t in range(2):
        a = jnp.zeros((PAGE, CMP_HIDDEN), F32)
        b = jnp.zeros((PAGE, CMP_HIDDEN), F32)
        for r in range(CMP_STRIDE):
            xr = cmp_scr[t, pl.ds(r, PAST // CMP_STRIDE, stride=CMP_STRIDE), :][:, 0:HEAD_DIM]
            a = a + _dot((xr + pos_ref[t, r:r + 1, :]).astype(BF16), w1_ref[t, r])
            b = b + _dot((xr + pos_ref[t, CMP_STRIDE + r:CMP_STRIDE + r + 1, :]).astype(BF16),
                         w1_ref[t, CMP_STRIDE + r])
        hid = a + pltpu.roll(b, PAGE - 1, 0)
        kvc.append(_dot(_gelu_tanh(hid).astype(BF16), w2_ref[t]).astype(BF16))
    kc, vc = kvc
    n_cmp = PAST // CMP_STRIDE - 1

    qn = qn_ref[0]
    slope_n = _row_const(NSA_SLOPES)
    dist_c = float(PAST - (CMP_LEN - 1)) - float(CMP_STRIDE) * lane8.astype(F32)
    mask_c = lane8 < n_cmp
    s_c = jnp.where(mask_c, _dot_nt(qn, kc) - slope_n * dist_c, NEG)
    p_c = jnp.where(mask_c, jnp.exp(s_c - jnp.max(s_c, axis=1, keepdims=True)), 0.0)
    p_c = p_c / jnp.maximum(jnp.sum(p_c, axis=1, keepdims=True), 1e-30)
    p_cb = jnp.where(live, p_c, 0.0).astype(BF16)
    o_c = _dot(p_cb, vc)
    imp = jnp.sum(_dot(p_cb, cover_ref[...]), axis=0, keepdims=True)

    q_blk = PAST // SEL_BLOCK
    lane1 = lane8[0:1, :]
    forced = (lane1 == 0) | (lane1 > q_blk - N_LOCAL_SEL)
    score = jnp.where(lane1 <= q_blk, jnp.where(forced, -NEG, imp), NEG)
    sel = _rank_select(jnp.broadcast_to(score, (8, 128)), q_blk + 1, N_SEL)
    sel = jnp.where(lane8 <= q_blk, sel, 0.0)
    selm = _dot(sel.astype(BF16), es_ref[...]) > 0.5
    z = s_scr[8 * G_NSA:8 * G_NSA + 8, :]
    s_s = jnp.where(selm, z - slope_n * dist_f, NEG)
    zn = z_new[8 * G_NSA:8 * G_NSA + 8, :]
    m = jnp.maximum(jnp.max(s_s, axis=1, keepdims=True), zn)
    p = jnp.where(selm, jnp.exp(s_s - m), 0.0)
    pn = jnp.exp(zn - m)
    p_scr[8 * G_NSA:8 * G_NSA + 8, :] = p.astype(BF16)
    pnew.append(pn)
    inv.append(1.0 / jnp.maximum(jnp.sum(p, axis=1, keepdims=True) + pn, 1e-30))

    wl = lax.broadcasted_iota(jnp.int32, (8, WINDOW), 1)
    mask_w = wl >= 1
    wn = wn_ref[0]
    s_w = jnp.where(mask_w, _dot_nt(qn, wkb_ref[0].astype(BF16)) - slope_n * (float(WINDOW) - wl.astype(F32)), NEG)
    z_wn = jnp.sum(qn.astype(F32) * wn[:, 0:HEAD_DIM].astype(BF16).astype(F32), axis=1, keepdims=True)
    m = jnp.maximum(jnp.max(s_w, axis=1, keepdims=True), z_wn)
    p = jnp.where(mask_w, jnp.exp(s_w - m), 0.0)
    pn = jnp.exp(z_wn - m)
    o_w = _dot(p.astype(BF16), wvb_ref[0].astype(BF16)) + pn * wn[:, HEAD_DIM:2 * HEAD_DIM].astype(BF16).astype(F32)
    o_w = o_w / jnp.maximum(jnp.sum(p, axis=1, keepdims=True) + pn, 1e-30)

    n_blk = PAST // MOBA_BLOCK
    per_blk = MOBA_BLOCK // PAGE
    rowk = lax.broadcasted_iota(jnp.int32, (8, KV_COLS), 0)
    kmean = jnp.zeros((8, KV_COLS), F32)
    for n in range(n_blk):
        kmean = jnp.where(rowk == n, sum(ksum[n * per_blk:(n + 1) * per_blk]) * (1.0 / MOBA_BLOCK), kmean)
    kmean = jnp.concatenate([kmean, jnp.zeros((128 - 8, KV_COLS), F32)], axis=0).astype(BF16)
    gate = _dot_nt(qm[8 * G_MOBA:8 * G_MOBA + 8, :], kmean)
    gate = jnp.where(lane8 < n_blk, gate, NEG)
    selb = _rank_select(gate, n_blk, MOBA_TOPK)
    z = s_scr[8 * G_MOBA:8 * G_MOBA + 8, :]
    slope_m = _row_const(MOBA_SLOPES)
    sel_cols = jnp.concatenate([jnp.broadcast_to(selb[:, n:n + 1], (8, MOBA_BLOCK)) for n in range(n_blk)], axis=1)
    selk = sel_cols > 0.5
    s_m = jnp.where(selk, z - slope_m * dist_f, NEG)
    zn = z_new[8 * G_MOBA:8 * G_MOBA + 8, :]
    m = jnp.maximum(jnp.max(s_m, axis=1, keepdims=True), zn)
    p = jnp.where(selk, jnp.exp(s_m - m), 0.0)
    pn = jnp.exp(zn - m)
    p_scr[8 * G_MOBA:8 * G_MOBA + 8, :] = p.astype(BF16)
    pnew.append(pn)
    inv.append(1.0 / jnp.maximum(jnp.sum(p, axis=1, keepdims=True) + pn, 1e-30))

    lf = jnp.concatenate([pg[0] for pg in lfp], axis=1)
    bias = _suffix_sum_excl(lf, u, 3) + hs[:, 3:4]
    z = s_scr[8 * G_FOX:8 * G_FOX + 8, :]
    s_f = z + bias
    zn = z_new[8 * G_FOX:8 * G_FOX + 8, :]
    m = jnp.maximum(jnp.max(s_f, axis=1, keepdims=True), zn)
    p = jnp.exp(s_f - m)
    pn = jnp.exp(zn - m)
    p_scr[8 * G_FOX:8 * G_FOX + 8, :] = p.astype(BF16)
    pnew.append(pn)
    inv.append(1.0 / jnp.maximum(jnp.sum(p, axis=1, keepdims=True) + pn, 1e-30))

    res = jnp.concatenate(pnew, axis=0) * vn_ref[0].astype(BF16).astype(F32)
    for j in range(N_PAGES):
        res = res + _dot(p_scr[:, j * PAGE:(j + 1) * PAGE], vp[j][0].astype(BF16))
    res = res * jnp.concatenate(inv, axis=0)

    def head_lanes(g, kv):
        return res[8 * g:8 * g + 8, kv * HEAD_DIM:(kv + 1) * HEAD_DIM]

    gsig = _sigmoid(hs)
    o_nsa = gsig[:, 0:1] * o_c + gsig[:, 1:2] * head_lanes(G_NSA, SEL_KV) + gsig[:, 2:3] * o_w
    pieces = [head_lanes(G_SB, h)[h:h + 1, :] for h in range(HEADS)]
    pieces += [o_nsa[h:h + 1, :] for h in range(HEADS)]
    pieces += [head_lanes(G_MOBA, MOBA_KV0 + h)[h:h + 1, :] for h in range(HEADS)]
    pieces += [head_lanes(G_FOX, FOX_KV0 + h)[h:h + 1, :] for h in range(HEADS)]
    o_ref[0] = jnp.concatenate(pieces, axis=1).astype(BF16)


def _query_rows(q):
    b = q.shape[0]
    out = jnp.zeros((b, Q_ROWS, KV_COLS), q.dtype)
    kv_of = {G_SB: lambda h: h, G_NSA: lambda h: SEL_KV, G_MOBA: lambda h: MOBA_KV0 + h, G_FOX: lambda h: FOX_KV0 + h}
    for g in range(4):
        for h in range(HEADS):
            kv = kv_of[g](h)
            qh = q[:, (g * HEADS + h) * HEAD_DIM:(g * HEADS + h + 1) * HEAD_DIM]
            out = out.at[:, 8 * g + h, kv * HEAD_DIM:(kv + 1) * HEAD_DIM].set(qh)
    return out


def decode_attention(q, k_new, v_new, misc, ck, cv, clf, wk_buf, wv_buf, page_table, cmp):
    b = q.shape[0]
    pos_k, w1k, w2k, pos_v, w1v, w2v = cmp
    qm = _query_rows(q)
    qn = jnp.pad(q[:, BRANCH_WIDTH:2 * BRANCH_WIDTH].reshape(b, HEADS, HEAD_DIM), ((0, 0), (0, 8 - HEADS), (0, 0)))
    gates = misc[:, 128:128 + 3 * HEADS].reshape(b, HEADS, 3)
    lf_new = misc[:, 128 + FGT_LANE0:128 + FGT_LANE0 + HEADS].reshape(b, HEADS, 1)
    hs = jnp.pad(jnp.concatenate([gates, lf_new], axis=2), ((0, 0), (0, 8 - HEADS), (0, 124)))
    wn = misc[:, 0:128].reshape(b, 1, 128)
    pos = jnp.stack([pos_k, pos_v])
    w1 = jnp.stack([w1k, w1v]).reshape(2, CMP_LEN, HEAD_DIM, CMP_HIDDEN).astype(BF16)
    w2 = jnp.stack([w2k, w2v]).astype(BF16)

    def page_map(j):
        return lambda i, pt: (pt[i, j], 0, 0)

    per_b3 = lambda i, pt: (i, 0, 0)
    const2 = lambda i, pt: (0, 0)
    const3 = lambda i, pt: (0, 0, 0)
    const4 = lambda i, pt: (0, 0, 0, 0)
    in_specs = ([pl.BlockSpec((1, PAGE, KV_COLS), page_map(j)) for j in range(N_PAGES)]
                + [pl.BlockSpec((1, PAGE, KV_COLS), page_map(j)) for j in range(N_PAGES)]
                + [pl.BlockSpec((1, 8, PAGE), page_map(j)) for j in range(N_PAGES)]
                + [pl.BlockSpec((1, Q_ROWS, KV_COLS), per_b3),
                   pl.BlockSpec((1, 8, HEAD_DIM), per_b3),
                   pl.BlockSpec((1, 1, KV_COLS), per_b3),
                   pl.BlockSpec((1, 1, KV_COLS), per_b3),
                   pl.BlockSpec((1, 8, 128), per_b3),
                   pl.BlockSpec((1, 1, 128), per_b3),
                   pl.BlockSpec((1, WINDOW, HEAD_DIM), per_b3),
                   pl.BlockSpec((1, WINDOW, HEAD_DIM), per_b3),
                   pl.BlockSpec((2, CMP_LEN, HEAD_DIM), const3),
                   pl.BlockSpec((2, CMP_LEN, HEAD_DIM, CMP_HIDDEN), const4),
                   pl.BlockSpec((2, CMP_HIDDEN, HEAD_DIM), const3),
                   pl.BlockSpec((SUM_CHUNK, SUM_CHUNK), const2),
                   pl.BlockSpec((128, 128), const2),
                   pl.BlockSpec((128, PAST), const2)])
    grid_spec = pltpu.PrefetchScalarGridSpec(
        num_scalar_prefetch=1,
        grid=(b,),
        in_specs=in_specs,
        out_specs=pl.BlockSpec((1, 1, D_MODEL), per_b3),
        scratch_shapes=[pltpu.VMEM((Q_ROWS, PAST), F32), pltpu.VMEM((Q_ROWS, PAST), BF16),
                        pltpu.VMEM((2, PAST, 128), F32)],
    )
    out = pl.pallas_call(
        _decode_body,
        grid_spec=grid_spec,
        out_shape=jax.ShapeDtypeStruct((b, 1, D_MODEL), BF16),
        compiler_params=_cparams(("arbitrary",)),
        name="decode_attention",
    )(page_table, *([ck] * N_PAGES), *([cv] * N_PAGES), *([clf] * N_PAGES),
      qm, qn, k_new.reshape(b, 1, KV_COLS), v_new.reshape(b, 1, KV_COLS), hs, wn, wk_buf, wv_buf,
      pos, w1, w2, _strict_upper_sum_matrix(SUM_CHUNK), _cover_matrix()[:128], _block_expand_matrix(SEL_BLOCK, PAST))
    return out.reshape(b, D_MODEL)


def sample_mixers(x, g_mix, w_parts, cmp, wb, wg, wo, ck, cv, clf, wk_buf, wv_buf, page_table):
    b = x.shape[0]
    wq, wk, wv, wm, bias = w_parts
    h, q, k, v, _, _, misc = mixer_project(x, g_mix, wq, wk, wv, wm, bias, tm=b, seq=1, prompt=False)
    o = decode_attention(q, k, v, misc, ck, cv, clf, wk_buf, wv_buf, page_table, cmp)
    outs = [o[:, i * BRANCH_WIDTH:(i + 1) * BRANCH_WIDTH] for i in range(4)]
    x = merge_branches(x, h, outs, wb, wg, wo, tm=b)
    kw = misc[:, 0:HEAD_DIM].reshape(b, 1, 1, HEAD_DIM)
    vw = misc[:, HEAD_DIM:2 * HEAD_DIM].reshape(b, 1, 1, HEAD_DIM)
    state = (k.reshape(b, 1, N_KV_HEADS, HEAD_DIM), v.reshape(b, 1, N_KV_HEADS, HEAD_DIM),
             misc[:, 128 + FGT_LANE0:128 + FGT_LANE0 + HEADS].reshape(b, 1, HEADS),
             jnp.concatenate([wk_buf[:, 1:].reshape(b, WINDOW - 1, 1, HEAD_DIM), kw], axis=1),
             jnp.concatenate([wv_buf[:, 1:].reshape(b, WINDOW - 1, 1, HEAD_DIM), vw], axis=1))
    return x, state


PROMPT_TM = 512


def kernel(x_prompt, x_sample, cache_k, cache_v, cache_logf, state_win_k, state_win_v, page_table,
           g_ffn1, w_ffn1_gate, w_ffn1_up, w_ffn1_down, g_mix, w_in, b_fgt,
           cmp_pos_k, cmp_w1_k, cmp_w2_k, cmp_pos_v, cmp_w1_v, cmp_w2_v,
           w_branch, w_merge_gate, w_out, g_ffn2, w_ffn2_gate, w_ffn2_up, w_ffn2_down, g_final):
    B, S, d = x_prompt.shape
    depth = g_ffn1.shape[0]
    nb = x_sample.shape[0]
    n_phys = cache_k.shape[1]
    xp = x_prompt.reshape(B * S, d)
    xs = x_sample.reshape(nb, d)
    pst = [[] for _ in range(5)]
    sst = [[] for _ in range(5)]
    ck_all = cache_k.reshape(depth, n_phys, PAGE, KV_COLS)
    cv_all = cache_v.reshape(depth, n_phys, PAGE, KV_COLS)
    clf_all = jnp.pad(jnp.swapaxes(cache_logf, 2, 3), ((0, 0), (0, 0), (0, 8 - HEADS), (0, 0)))
    for l in range(depth):
        last = l == depth - 1
        f1 = (w_ffn1_gate[l].astype(BF16), w_ffn1_up[l].astype(BF16), w_ffn1_down[l].astype(BF16))
        f2 = (w_ffn2_gate[l].astype(BF16), w_ffn2_up[l].astype(BF16), w_ffn2_down[l].astype(BF16))
        w_parts = _split_w_in(w_in[l], b_fgt[l])
        cmp = (cmp_pos_k[l], cmp_w1_k[l], cmp_w2_k[l], cmp_pos_v[l], cmp_w1_v[l], cmp_w2_v[l])
        wb, wg, wo = w_branch[l].astype(BF16), w_merge_gate[l].astype(BF16), w_out[l].astype(BF16)
        xp = ffn_half(xp, g_ffn1[l], *f1, tm=PROMPT_TM)
        xp, st_p = prompt_mixers(xp, g_mix[l], w_parts, cmp, wb, wg, wo, batch=B, seq=S, tm=PROMPT_TM)
        xp = ffn_half(xp, g_ffn2[l], *f2, g_final if last else None, tm=PROMPT_TM)
        xs = ffn_half(xs, g_ffn1[l], *f1, tm=nb)
        xs, st_s = sample_mixers(xs, g_mix[l], w_parts, cmp, wb, wg, wo, ck_all[l], cv_all[l], clf_all[l],
                                 state_win_k[l].reshape(nb, WINDOW, HEAD_DIM),
                                 state_win_v[l].reshape(nb, WINDOW, HEAD_DIM), page_table)
        xs = ffn_half(xs, g_ffn2[l], *f2, g_final if last else None, tm=nb)
        for i in range(5):
            pst[i].append(st_p[i])
            sst[i].append(st_s[i])
    y_prompt = xp.reshape(B, S, d)
    y_sample = xs.reshape(nb, 1, d)
    return (y_prompt, y_sample) + tuple(jnp.stack(p) for p in pst) + tuple(jnp.stack(p) for p in sst)
```

```python
import functools

import numpy as np
import jax
import jax.numpy as jnp
from jax import lax
from jax.experimental import pallas as pl
from jax.experimental.pallas import tpu as pltpu

F32 = jnp.float32
BF16 = jnp.bfloat16

D_MODEL = 1024
HEAD_DIM = 64
HEADS = 4
BRANCH_WIDTH = HEADS * HEAD_DIM
N_KV_HEADS = 14
KV_COLS = N_KV_HEADS * HEAD_DIM
CMP_KV = 4
SEL_KV = 5
MOBA_KV0 = 6
FOX_KV0 = 10
CMP_STRIDE = 16
CMP_LEN = 32
CMP_HIDDEN = 128
SEL_BLOCK = 64
N_SEL = 16
N_LOCAL_SEL = 2
WINDOW = 512
MOBA_BLOCK = 256
MOBA_TOPK = 3
D_FF = 2816
RMS_EPS = 1e-6
Q_SCALE = HEAD_DIM ** -0.5
NEG = -1e30
MISC_COLS = 256
FGT_LANE0 = 12
NSA_SLOPES = (2.0 ** -1, 2.0 ** -3, 2.0 ** -5, 2.0 ** -7)
MOBA_SLOPES = (2.0 ** -2, 2.0 ** -4, 2.0 ** -6, 2.0 ** -8)
SB_FLOOR = -110.0

VMEM_LIMIT = 56 * 1024 * 1024


def _cparams(sem):
    return pltpu.CompilerParams(dimension_semantics=sem, vmem_limit_bytes=VMEM_LIMIT)


def _dot(a, b):
    return jnp.dot(a, b, preferred_element_type=F32)


def _dot_nt(a, b):
    return lax.dot_general(a, b, (((1,), (1,)), ((), ())), preferred_element_type=F32)


def _rms(x, g):
    return x * lax.rsqrt(jnp.mean(x * x, axis=-1, keepdims=True) + RMS_EPS) * g


def _softplus(z):
    return jnp.maximum(z, 0.0) + jnp.log(1.0 + jnp.exp(-jnp.abs(z)))


def _sigmoid(z):
    return 1.0 / (1.0 + jnp.exp(-z))


def _split2(a):
    hi = a.astype(BF16)
    lo = (a - hi.astype(F32)).astype(BF16)
    return hi, lo


def _split3(a):
    a1 = a.astype(BF16)
    r1 = a - a1.astype(F32)
    a2 = r1.astype(BF16)
    a3 = (r1 - a2.astype(F32)).astype(BF16)
    return a1, a2, a3


def _ffn_body(x_ref, g_ref, wg_ref, wu_ref, wd_ref, gf_ref, o_ref, h_scr, acc_scr, *, n_f, final_norm):
    j = pl.program_id(1)

    @pl.when(j == 0)
    def _():
        h_scr[...] = _rms(x_ref[...], g_ref[...]).astype(BF16)
        acc_scr[...] = jnp.zeros_like(acc_scr)

    h = h_scr[...]
    a = _dot(h, wg_ref[...])
    u = _dot(h, wu_ref[...])
    act = (a * _sigmoid(a)) * u
    acc_scr[...] += _dot(act.astype(BF16), wd_ref[...])

    @pl.when(j == n_f - 1)
    def _():
        y = x_ref[...] + 0.5 * acc_scr[...]
        if final_norm:
            y = _rms(y, gf_ref[...])
        o_ref[...] = y


def ffn_half(x, g, wg, wu, wd, g_final=None, *, tm, tf=1408):
    m, d = x.shape
    f = wg.shape[1]
    n_f = f // tf
    final_norm = g_final is not None
    gf = g_final if final_norm else g
    body = functools.partial(_ffn_body, n_f=n_f, final_norm=final_norm)
    return pl.pallas_call(
        body,
        grid=(m // tm, n_f),
        in_specs=[
            pl.BlockSpec((tm, d), lambda i, j: (i, 0)),
            pl.BlockSpec((1, d), lambda i, j: (0, 0)),
            pl.BlockSpec((d, tf), lambda i, j: (0, j)),
            pl.BlockSpec((d, tf), lambda i, j: (0, j)),
            pl.BlockSpec((tf, d), lambda i, j: (j, 0)),
            pl.BlockSpec((1, d), lambda i, j: (0, 0)),
        ],
        out_specs=pl.BlockSpec((tm, d), lambda i, j: (i, 0)),
        out_shape=jax.ShapeDtypeStruct((m, d), F32),
        scratch_shapes=[pltpu.VMEM((tm, d), BF16), pltpu.VMEM((tm, d), F32)],
        compiler_params=_cparams(("parallel", "arbitrary")),
        name="ffn_half",
    )(x, g.reshape(1, d), wg, wu, wd, gf.reshape(1, d))


def _misc_and_logf(h, wm_ref, bias_ref, misc_ref):
    mz = _dot(h, wm_ref[...])
    grp = mz[:, 128:256]
    lane = lax.broadcasted_iota(jnp.int32, grp.shape, 1)
    is_f = (lane >= FGT_LANE0) & (lane < FGT_LANE0 + HEADS)
    xf = grp + bias_ref[...]
    lf = jnp.minimum(xf, 0.0) - jnp.log(1.0 + jnp.exp(-jnp.abs(xf)))
    lf = jnp.where(is_f, lf, 0.0)
    misc_ref[:, 0:128] = mz[:, 0:128]
    misc_ref[:, 128:256] = jnp.where(is_f, lf, grp)
    return lf


def _proj_prompt_body(x_ref, g_ref, wq_ref, wkT_ref, wvT_ref, wm_ref, bias_ref, tri_ref,
                      h_ref, q_ref, kT_ref, vT_ref, kTb_ref, vTb_ref, misc_ref, cum_ref, kmT_ref, carry_scr,
                      *, tm, tiles_per_seq):
    t = pl.program_id(0) % tiles_per_seq
    h = _rms(x_ref[...], g_ref[...]).astype(BF16)
    h_ref[...] = h
    q_ref[...] = (_dot(h, wq_ref[...]) * Q_SCALE).astype(BF16)
    kT = _dot_nt(wkT_ref[...], h)
    kT_ref[0] = kT
    kTb_ref[0] = kT.astype(BF16)
    vT = _dot_nt(wvT_ref[...], h)
    vT_ref[0] = vT
    vTb_ref[0] = vT.astype(BF16)
    lf = _misc_and_logf(h, wm_ref, bias_ref, misc_ref)

    @pl.when(t == 0)
    def _():
        carry_scr[...] = jnp.zeros_like(carry_scr)
        kmT_ref[...] = jnp.zeros_like(kmT_ref)

    a1, a2, a3 = _split3(lf)
    tri = tri_ref[...]
    cs = _dot(tri, a1) + _dot(tri, a2) + _dot(tri, a3) + carry_scr[0:1, :]
    cum_ref[...] = cs
    carry_scr[0:1, :] = cs[tm - 1:tm, :]

    lane = lax.broadcasted_iota(jnp.int32, (BRANCH_WIDTH, 128), 1)
    km = kmT_ref[0]
    for r in range(tm // MOBA_BLOCK):
        kblk = kT[MOBA_KV0 * HEAD_DIM:(MOBA_KV0 + HEADS) * HEAD_DIM, r * MOBA_BLOCK:(r + 1) * MOBA_BLOCK]
        col = jnp.sum(kblk, axis=1, keepdims=True) * (1.0 / MOBA_BLOCK)
        km = jnp.where(lane == t * (tm // MOBA_BLOCK) + r, col, km)
    kmT_ref[0] = km


def mixer_project_prompt(x, g, wq, wkT, wvT, wm, bias, *, tm, batch, seq):
    m, d = x.shape
    tps = seq // tm
    tri = jnp.asarray(np.tril(np.ones((tm, tm), np.float32)), BF16)
    row = lambda i: (i, 0)
    const = lambda i: (0, 0)
    featmaj = lambda i: (i // tps, 0, i % tps)
    kv_f32 = jax.ShapeDtypeStruct((batch, KV_COLS, seq), F32)
    kv_b16 = jax.ShapeDtypeStruct((batch, KV_COLS, seq), BF16)
    body = functools.partial(_proj_prompt_body, tm=tm, tiles_per_seq=tps)
    return pl.pallas_call(
        body,
        grid=(m // tm,),
        in_specs=[
            pl.BlockSpec((tm, d), row),
            pl.BlockSpec((1, d), const),
            pl.BlockSpec((d, d), const),
            pl.BlockSpec((KV_COLS, d), const),
            pl.BlockSpec((KV_COLS, d), const),
            pl.BlockSpec((d, MISC_COLS), const),
            pl.BlockSpec((1, 128), const),
            pl.BlockSpec((tm, tm), const),
        ],
        out_specs=[
            pl.BlockSpec((tm, d), row), pl.BlockSpec((tm, d), row),
            pl.BlockSpec((1, KV_COLS, tm), featmaj), pl.BlockSpec((1, KV_COLS, tm), featmaj),
            pl.BlockSpec((1, KV_COLS, tm), featmaj), pl.BlockSpec((1, KV_COLS, tm), featmaj),
            pl.BlockSpec((tm, MISC_COLS), row),
            pl.BlockSpec((tm, 128), row),
            pl.BlockSpec((1, BRANCH_WIDTH, 128), lambda i: (i // tps, 0, 0)),
        ],
        out_shape=[
            jax.ShapeDtypeStruct((m, d), BF16),
            jax.ShapeDtypeStruct((m, d), BF16),
            kv_f32, kv_f32, kv_b16, kv_b16,
            jax.ShapeDtypeStruct((m, MISC_COLS), F32),
            jax.ShapeDtypeStruct((m, 128), F32),
            jax.ShapeDtypeStruct((batch, BRANCH_WIDTH, 128), F32),
        ],
        scratch_shapes=[pltpu.VMEM((8, 128), F32)],
        compiler_params=_cparams(("arbitrary",)),
        name="mixer_project_prompt",
    )(x, g.reshape(1, d), wq, wkT, wvT, wm, bias, tri)


def _proj_sample_body(x_ref, g_ref, wq_ref, wkT_ref, wvT_ref, wm_ref, bias_ref,
                      h_ref, q_ref, k_ref, v_ref, misc_ref):
    h = _rms(x_ref[...], g_ref[...]).astype(BF16)
    h_ref[...] = h
    q_ref[...] = (_dot(h, wq_ref[...]) * Q_SCALE).astype(BF16)
    k_ref[...] = _dot_nt(h, wkT_ref[...])
    v_ref[...] = _dot_nt(h, wvT_ref[...])
    _misc_and_logf(h, wm_ref, bias_ref, misc_ref)


def mixer_project_sample(x, g, wq, wkT, wvT, wm, bias):
    m, d = x.shape
    row = lambda i: (0, 0)
    full = lambda shape: pl.BlockSpec(shape, row)
    return pl.pallas_call(
        _proj_sample_body,
        grid=(1,),
        in_specs=[full((m, d)), full((1, d)), full((d, d)), full((KV_COLS, d)), full((KV_COLS, d)),
                  full((d, MISC_COLS)), full((1, 128))],
        out_specs=[full((m, d)), full((m, d)), full((m, KV_COLS)), full((m, KV_COLS)), full((m, MISC_COLS))],
        out_shape=[
            jax.ShapeDtypeStruct((m, d), BF16),
            jax.ShapeDtypeStruct((m, d), BF16),
            jax.ShapeDtypeStruct((m, KV_COLS), F32),
            jax.ShapeDtypeStruct((m, KV_COLS), F32),
            jax.ShapeDtypeStruct((m, MISC_COLS), F32),
        ],
        compiler_params=_cparams(("arbitrary",)),
        name="mixer_project_sample",
    )(x, g.reshape(1, d), wq, wkT, wvT, wm, bias)


def _gelu_tanh(x):
    return 0.5 * x * (1.0 + jnp.tanh(np.sqrt(2.0 / np.pi).astype(np.float32) * (x + 0.044715 * (x * x * x))))


def _compress_body(xs_ref, pos_ref, w1a_ref, w1b_ref, w2_ref, o_ref):
    xs = xs_ref[0, 0]
    n16 = xs.shape[0]
    pos = pos_ref[0]
    a = _dot((xs + pos[0:1, :]).astype(BF16), w1a_ref[0])
    b = _dot((xs + pos[1:2, :]).astype(BF16), w1b_ref[0])
    hid = a + pltpu.roll(b, n16 - 1, 0)
    o_ref[0, 0] = _dot(_gelu_tanh(hid).astype(BF16), w2_ref[0])


def nsa_compress(xs, pos, w1, w2):
    _, b, n16, cw = xs.shape
    pos2 = pos.reshape(2, 2, cw)
    w1a = w1[:, :cw].astype(BF16)
    w1b = w1[:, cw:].astype(BF16)
    return pl.pallas_call(
        _compress_body,
        grid=(2, b),
        in_specs=[
            pl.BlockSpec((1, 1, n16, cw), lambda t, i: (t, i, 0, 0)),
            pl.BlockSpec((1, 2, cw), lambda t, i: (t, 0, 0)),
            pl.BlockSpec((1, cw, CMP_HIDDEN), lambda t, i: (t, 0, 0)),
            pl.BlockSpec((1, cw, CMP_HIDDEN), lambda t, i: (t, 0, 0)),
            pl.BlockSpec((1, CMP_HIDDEN, HEAD_DIM), lambda t, i: (t, 0, 0)),
        ],
        out_specs=pl.BlockSpec((1, 1, n16, HEAD_DIM), lambda t, i: (t, i, 0, 0)),
        out_shape=jax.ShapeDtypeStruct((2, b, n16, HEAD_DIM), F32),
        compiler_params=_cparams(("parallel", "parallel")),
        name="nsa_compress",
    )(xs, pos2, w1a, w1b, w2.astype(BF16))


TQ = 256
TK = 256


def _head_spec(s, head):
    return pl.BlockSpec((1, HEAD_DIM, s), lambda i, j, head=head: (i, head, 0))


def _pair_spec(s, pair):
    return pl.BlockSpec((1, 2 * HEAD_DIM, s), lambda i, j, pair=pair: (i, pair, 0))


def _merge_pairs(accs):
    lane = lax.broadcasted_iota(jnp.int32, accs[0].shape, 1)
    lo = lane < HEAD_DIM
    return jnp.concatenate([jnp.where(lo, accs[0], accs[1]), jnp.where(lo, accs[2], accs[3])], axis=1)


def _wide(x):
    return jnp.concatenate([x] * (TK // 128), axis=1)


def _online(s, mask, m, l, acc, vT):
    m_new = jnp.maximum(m, jnp.max(s, axis=1, keepdims=True))
    alpha = jnp.exp(m - m_new)
    p = jnp.exp(s - _wide(m_new))
    if mask is not None:
        p = jnp.where(mask, p, 0.0)
    l = alpha * l + jnp.sum(p, axis=1, keepdims=True)
    acc = alpha * acc + _dot_nt(p.astype(BF16), vT)
    return m_new, l, acc


def _softmax_init():
    return tuple((jnp.full((TQ, 128), NEG, F32), jnp.zeros((TQ, 128), F32), jnp.zeros((TQ, 128), F32))
                 for _ in range(HEADS))


def _finish(l, acc):
    return acc / jnp.maximum(l, 1e-30)


def _tile_iotas(qi):
    row = qi * TQ + lax.broadcasted_iota(jnp.int32, (TQ, TK), 0)
    lcol = lax.broadcasted_iota(jnp.int32, (TQ, TK), 1)
    return row, lcol


def _sb_body(q_ref, k0_ref, k1_ref, k2_ref, k3_ref, v0_ref, v1_ref, u_ref, o_ref):
    qi = pl.program_id(1)
    k_refs = (k0_ref, k1_ref, k2_ref, k3_ref)
    v_refs = (v0_ref, v1_ref)
    row, lcol = _tile_iotas(qi)
    u = u_ref[...]
    qs = [q_ref[0, :, h * HEAD_DIM:(h + 1) * HEAD_DIM] for h in range(HEADS)]

    def tile(k0, before, accs, runs):
        new_a, new_r = [], []
        for h in range(HEADS):
            z = _dot(qs[h], k_refs[h][0, :, pl.ds(k0, TK)])
            sp = _softplus(z)
            lk = -sp if before is None else jnp.where(before, -sp, 0.0)
            hi, lo = _split2(lk)
            sums = _dot(hi, u) + _dot(lo, u)
            loc = sums[:, 0:TK]
            w = jnp.exp(z - sp + loc + _wide(runs[h]))
            if before is not None:
                w = jnp.where(before, w, 0.0)
            new_a.append(accs[h] + _dot_nt(w.astype(BF16), v_refs[h // 2][0, :, pl.ds(k0, TK)]))
            new_r.append(runs[h] + sums[:, TK:TK + 128])
        return tuple(new_a), tuple(new_r)

    zero_a = tuple(jnp.zeros((TQ, 128), F32) for _ in range(HEADS))
    zero_r = tuple(jnp.zeros((TQ, 128), F32) for _ in range(HEADS))
    q0 = pl.multiple_of(qi * TK, TK)
    accs, runs = tile(q0, (q0 + lcol) < row, zero_a, zero_r)

    def cond(c):
        jj, _, runs = c
        top = jnp.max(jnp.maximum(jnp.maximum(runs[0], runs[1]), jnp.maximum(runs[2], runs[3])))
        return jnp.logical_and(jj <= qi, top > SB_FLOOR)

    def body(c):
        jj, accs, runs = c
        accs, runs = tile(pl.multiple_of((qi - jj) * TK, TK), None, accs, runs)
        return jj + 1, accs, runs

    _, accs, _ = lax.while_loop(cond, body, (jnp.int32(1), accs, runs))
    o_ref[0] = _merge_pairs(accs).astype(BF16)


def _strict_upper_sum_matrix(n, total_cols=0):
    u = np.concatenate([np.tril(np.ones((n, n), np.float32), -1), np.ones((n, total_cols), np.float32)], axis=1)
    return jnp.asarray(u, BF16)


def sb_attention(q, kTb, vTb):
    b, s, _ = q.shape
    return pl.pallas_call(
        _sb_body,
        grid=(b, s // TQ),
        in_specs=[pl.BlockSpec((1, TQ, BRANCH_WIDTH), lambda i, j: (i, j, 0))]
        + [_head_spec(s, h) for h in range(HEADS)]
        + [_pair_spec(s, 0), _pair_spec(s, 1)]
        + [pl.BlockSpec((TK, TK + 128), lambda i, j: (0, 0))],
        out_specs=pl.BlockSpec((1, TQ, BRANCH_WIDTH), lambda i, j: (i, j, 0)),
        out_shape=jax.ShapeDtypeStruct((b, s, BRANCH_WIDTH), BF16),
        compiler_params=_cparams(("parallel", "arbitrary")),
        name="sb_attention",
    )(q, kTb, kTb, kTb, kTb, vTb, vTb, _strict_upper_sum_matrix(TK, 128))


def _fox_body(q_ref, k0_ref, k1_ref, k2_ref, k3_ref, v0_ref, v1_ref, cq_ref, ck_ref, o_ref):
    qi = pl.program_id(1)
    k_refs = (k0_ref, k1_ref, k2_ref, k3_ref)
    v_refs = (v0_ref, v1_ref)
    row, lcol = _tile_iotas(qi)
    qs = [q_ref[0, :, h * HEAD_DIM:(h + 1) * HEAD_DIM] for h in range(HEADS)]
    cqs = [jnp.broadcast_to(cq_ref[0, :, FGT_LANE0 + h:FGT_LANE0 + h + 1], (TQ, 128)) for h in range(HEADS)]

    def tile(k0, mask, carry):
        out = []
        for h in range(HEADS):
            m, l, acc = carry[h]
            z = _dot(qs[h], k_refs[h][0, :, pl.ds(k0, TK)])
            s = z + _wide(cqs[h]) - ck_ref[0, h:h + 1, pl.ds(k0, TK)]
            if mask is not None:
                s = jnp.where(mask, s, NEG)
            out.append(_online(s, mask, m, l, acc, v_refs[h // 2][0, :, pl.ds(k0, TK)]))
        return tuple(out)

    carry = lax.fori_loop(0, qi, lambda j, c: tile(pl.multiple_of(j * TK, TK), None, c), _softmax_init())
    q0 = pl.multiple_of(qi * TK, TK)
    carry = tile(q0, (q0 + lcol) <= row, carry)
    o_ref[0] = _merge_pairs([_finish(l, acc) for (_, l, acc) in carry]).astype(BF16)


def fox_attention(q, kTb, vTb, cum, cumT):
    b, s, _ = q.shape
    return pl.pallas_call(
        _fox_body,
        grid=(b, s // TQ),
        in_specs=[pl.BlockSpec((1, TQ, BRANCH_WIDTH), lambda i, j: (i, j, 3))]
        + [_head_spec(s, FOX_KV0 + h) for h in range(HEADS)]
        + [_pair_spec(s, FOX_KV0 // 2), _pair_spec(s, FOX_KV0 // 2 + 1)]
        + [pl.BlockSpec((1, TQ, 128), lambda i, j: (i, j, 0)),
           pl.BlockSpec((1, 8, s), lambda i, j: (i, 0, 0))],
        out_specs=pl.BlockSpec((1, TQ, BRANCH_WIDTH), lambda i, j: (i, j, 0)),
        out_shape=jax.ShapeDtypeStruct((b, s, BRANCH_WIDTH), BF16),
        compiler_params=_cparams(("parallel", "arbitrary")),
        name="fox_attention",
    )(q, kTb, kTb, kTb, kTb, vTb, vTb, cum, cumT)


def _rank_select(score, n_cols, k):
    lane = lax.broadcasted_iota(jnp.int32, score.shape, 1)
    rank = jnp.zeros(score.shape, F32)
    for c in range(n_cols):
        col = score[:, c:c + 1]
        ahead = jnp.where(col > score, 1.0, jnp.where(col == score, jnp.where(lane > c, 1.0, 0.0), 0.0))
        rank = rank + ahead
    return jnp.where(rank < k, 1.0, 0.0)


def _rank_select_rows(score, n_cols, k):
    n_pad = -(-n_cols // 8) * 8
    st = score.T[0:n_pad, :]
    cand = lax.broadcasted_iota(jnp.int32, st.shape, 0)
    rank = jnp.zeros(st.shape, F32)
    for c in range(n_cols):
        ref = st[c:c + 1, :]
        ahead = jnp.where(ref > st, 1.0, jnp.where(ref == st, jnp.where(cand > c, 1.0, 0.0), 0.0))
        rank = rank + ahead
    sel = jnp.where(rank < k, 1.0, 0.0)
    if n_pad < 128:
        sel = jnp.concatenate([sel, jnp.zeros((128 - n_pad, st.shape[1]), F32)], axis=0)
    return sel.T


def _moba_body(q_ref, k0_ref, k1_ref, k2_ref, k3_ref, v0_ref, v1_ref, kmT_ref, em_ref, o_ref):
    qi = pl.program_id(1)
    k_refs = (k0_ref, k1_ref, k2_ref, k3_ref)
    v_refs = (v0_ref, v1_ref)
    row, lcol = _tile_iotas(qi)
    lane = lax.broadcasted_iota(jnp.int32, (TQ, 128), 1)
    n_blocks = k0_ref.shape[2] // MOBA_BLOCK
    qs = [q_ref[0, :, h * HEAD_DIM:(h + 1) * HEAD_DIM] for h in range(HEADS)]
    sels = []
    for h in range(HEADS):
        gate = _dot(qs[h], kmT_ref[0, h * HEAD_DIM:(h + 1) * HEAD_DIM, :].astype(BF16))
        gate = jnp.where(lane < qi, gate, NEG)
        sel = _rank_select_rows(gate, n_blocks, MOBA_TOPK)
        sels.append(jnp.where(lane < qi, sel, 0.0).astype(BF16))

    def tile(k0, diag, carry):
        dist = row - (k0 + lcol)
        distf = dist.astype(F32)
        out = []
        for h in range(HEADS):
            m, l, acc = carry[h]
            z = _dot(qs[h], k_refs[h][0, :, pl.ds(k0, TK)])
            if diag:
                mask = dist >= 0
            else:
                mask = _dot(sels[h], em_ref[:, pl.ds(k0, TK)]) > 0.5
            s = jnp.where(mask, z - MOBA_SLOPES[h] * distf, NEG)
            out.append(_online(s, mask, m, l, acc, v_refs[h // 2][0, :, pl.ds(k0, TK)]))
        return tuple(out)

    carry = lax.fori_loop(0, qi, lambda j, c: tile(pl.multiple_of(j * TK, TK), False, c), _softmax_init())
    carry = tile(pl.multiple_of(qi * TK, TK), True, carry)
    o_ref[0] = _merge_pairs([_finish(l, acc) for (_, l, acc) in carry]).astype(BF16)


def _block_expand_matrix(block, s):
    e = (np.arange(128)[:, None] == (np.arange(s)[None, :] // block)).astype(np.float32)
    return jnp.asarray(e, BF16)


def moba_attention(q, kTb, vTb, kmT):
    b, s, _ = q.shape
    return pl.pallas_call(
        _moba_body,
        grid=(b, s // TQ),
        in_specs=[pl.BlockSpec((1, TQ, BRANCH_WIDTH), lambda i, j: (i, j, 2))]
        + [_head_spec(s, MOBA_KV0 + h) for h in range(HEADS)]
        + [_pair_spec(s, MOBA_KV0 // 2), _pair_spec(s, MOBA_KV0 // 2 + 1)]
        + [pl.BlockSpec((1, BRANCH_WIDTH, 128), lambda i, j: (i, 0, 0)),
           pl.BlockSpec((128, s), lambda i, j: (0, 0))],
        out_specs=pl.BlockSpec((1, TQ, BRANCH_WIDTH), lambda i, j: (i, j, 0)),
        out_shape=jax.ShapeDtypeStruct((b, s, BRANCH_WIDTH), BF16),
        compiler_params=_cparams(("parallel", "arbitrary")),
        name="moba_attention",
    )(q, kTb, kTb, kTb, kTb, vTb, vTb, kmT, _block_expand_matrix(MOBA_BLOCK, s))


def _nsa_body(q_ref, kT_ref, vT_ref, kcT_ref, vc_ref, kwT_ref, vwT_ref, g_ref, cover_ref, es_ref, o_ref, *, n_cmp):
    qi = pl.program_id(1)
    s_len = kT_ref.shape[2]
    n_sel_blocks = s_len // SEL_BLOCK
    row, lcol = _tile_iotas(qi)
    qs = [q_ref[0, :, h * HEAD_DIM:(h + 1) * HEAD_DIM] for h in range(HEADS)]

    ncp = kcT_ref.shape[2]
    crow = qi * TQ + lax.broadcasted_iota(jnp.int32, (TQ, ncp), 0)
    cn = lax.broadcasted_iota(jnp.int32, (TQ, ncp), 1)
    dist_c = crow - (cn * CMP_STRIDE + (CMP_LEN - 1))
    mask_c = jnp.where(cn < n_cmp, dist_c, -1) >= 0
    dist_cf = dist_c.astype(F32)
    kcT = kcT_ref[0]
    vc = vc_ref[0]
    cover = cover_ref[...]
    o_c = []
    imp = jnp.zeros((TQ, 128), F32)
    for h in range(HEADS):
        s = jnp.where(mask_c, _dot(qs[h], kcT) - NSA_SLOPES[h] * dist_cf, NEG)
        m = jnp.max(s, axis=1, keepdims=True)
        p = jnp.where(mask_c, jnp.exp(s - m), 0.0)
        p = p / jnp.maximum(jnp.sum(p, axis=1, keepdims=True), 1e-30)
        pb = p.astype(BF16)
        o_c.append(_dot(pb, vc))
        imp = imp + _dot(pb, cover)

    lane = lax.broadcasted_iota(jnp.int32, (TQ, 128), 1)
    q_blk = (qi * TQ + lax.broadcasted_iota(jnp.int32, (TQ, 128), 0)) // SEL_BLOCK
    forced = (lane == 0) | (lane > q_blk - N_LOCAL_SEL)
    score = jnp.where(lane <= q_blk, jnp.where(forced, -NEG, imp), NEG)
    sel = _rank_select_rows(score, n_sel_blocks, N_SEL)
    sel = jnp.where(lane <= q_blk, sel, 0.0).astype(BF16)

    def sel_tile(k0, diag, carry):
        kT = kT_ref[0, :, pl.ds(k0, TK)]
        vT = vT_ref[0, :, pl.ds(k0, TK)]
        dist = row - (k0 + lcol)
        selm = _dot(sel, es_ref[:, pl.ds(k0, TK)])
        if diag:
            mask = jnp.where(dist >= 0, selm, 0.0) > 0.5
        else:
            mask = selm > 0.5
        distf = dist.astype(F32)
        out = []
        for h in range(HEADS):
            m, l, acc = carry[h]
            s = jnp.where(mask, _dot(qs[h], kT) - NSA_SLOPES[h] * distf, NEG)
            out.append(_online(s, mask, m, l, acc, vT))
        return tuple(out)

    res = lax.fori_loop(0, qi, lambda j, c: sel_tile(pl.multiple_of(j * TK, TK), False, c), _softmax_init())
    res = sel_tile(pl.multiple_of(qi * TK, TK), True, res)
    o_s = [_finish(l, acc) for (_, l, acc) in res]

    def win_body(j, carry):
        k0 = pl.multiple_of(j * TK, TK)
        kT = kwT_ref[0, :, pl.ds(k0, TK)]
        vT = vwT_ref[0, :, pl.ds(k0, TK)]
        dist = row - (k0 + lcol)
        mask = jnp.where(dist >= 0, dist, WINDOW) < WINDOW
        distf = dist.astype(F32)
        out = []
        for h in range(HEADS):
            m, l, acc = carry[h]
            s = jnp.where(mask, _dot(qs[h], kT) - NSA_SLOPES[h] * distf, NEG)
            out.append(_online(s, mask, m, l, acc, vT))
        return tuple(out)

    res_w = lax.fori_loop(jnp.maximum(qi - WINDOW // TK, 0), qi + 1, win_body, _softmax_init())
    o_w = [_finish(l, acc)[:, 0:HEAD_DIM] for (_, l, acc) in res_w]

    g = _sigmoid(g_ref[0])
    outs = []
    for h in range(HEADS):
        g0 = g[:, 3 * h + 0:3 * h + 1]
        g1 = g[:, 3 * h + 1:3 * h + 2]
        g2 = g[:, 3 * h + 2:3 * h + 3]
        outs.append(g0 * o_c[h] + g1 * o_s[h][:, HEAD_DIM:2 * HEAD_DIM] + g2 * o_w[h])
    o_ref[0] = jnp.concatenate(outs, axis=1).astype(BF16)


def _cover_matrix():
    n = np.arange(256)[:, None] * CMP_STRIDE
    m = np.arange(128)[None, :] * SEL_BLOCK
    return jnp.asarray(((n < m + SEL_BLOCK) & (n + CMP_LEN > m)).astype(np.float32), BF16)


def nsa_attention(q, kTb, vTb, kcT, vc, kwT, vwT, misc, n_cmp):
    b, s, _ = q.shape
    ncp = kcT.shape[2]
    body = functools.partial(_nsa_body, n_cmp=n_cmp)
    return pl.pallas_call(
        body,
        grid=(b, s // TQ),
        in_specs=[
            pl.BlockSpec((1, TQ, BRANCH_WIDTH), lambda i, j: (i, j, 1)),
            _head_spec(s, SEL_KV),
            _pair_spec(s, SEL_KV // 2),
            pl.BlockSpec((1, HEAD_DIM, ncp), lambda i, j: (i, 0, 0)),
            pl.BlockSpec((1, ncp, HEAD_DIM), lambda i, j: (i, 0, 0)),
            pl.BlockSpec((1, HEAD_DIM, s), lambda i, j: (i, 0, 0)),
            pl.BlockSpec((1, 2 * HEAD_DIM, s), lambda i, j: (i, 0, 0)),
            pl.BlockSpec((1, TQ, 128), lambda i, j: (i, j, 1)),
            pl.BlockSpec((ncp, 128), lambda i, j: (0, 0)),
            pl.BlockSpec((128, s), lambda i, j: (0, 0)),
        ],
        out_specs=pl.BlockSpec((1, TQ, BRANCH_WIDTH), lambda i, j: (i, j, 0)),
        out_shape=jax.ShapeDtypeStruct((b, s, BRANCH_WIDTH), BF16),
        compiler_params=_cparams(("parallel", "arbitrary")),
        name="nsa_attention",
    )(q, kTb, vTb, kcT, vc, kwT, vwT, misc, _cover_matrix()[:ncp], _block_expand_matrix(SEL_BLOCK, s))


def _merge_body(x_ref, h_ref, o0_ref, o1_ref, o2_ref, o3_ref, wb_ref, wg_ref, wo_ref, out_ref):
    h = h_ref[...]
    d = x_ref.shape[1]
    mix = None
    for n, o_ref in enumerate((o0_ref, o1_ref, o2_ref, o3_ref)):
        gate = _sigmoid(_dot(h, wg_ref[:, n * d:(n + 1) * d]))
        term = gate * _dot(o_ref[...], wb_ref[n])
        mix = term if mix is None else mix + term
    out_ref[...] = x_ref[...] + _dot(mix.astype(BF16), wo_ref[...])


def merge_branches(x, h, outs, wb, wg, wo, *, tm):
    m, d = x.shape
    row = lambda i: (i, 0)
    return pl.pallas_call(
        _merge_body,
        grid=(m // tm,),
        in_specs=[
            pl.BlockSpec((tm, d), row),
            pl.BlockSpec((tm, d), row),
            pl.BlockSpec((tm, BRANCH_WIDTH), row),
            pl.BlockSpec((tm, BRANCH_WIDTH), row),
            pl.BlockSpec((tm, BRANCH_WIDTH), row),
            pl.BlockSpec((tm, BRANCH_WIDTH), row),
            pl.BlockSpec((4, BRANCH_WIDTH, d), lambda i: (0, 0, 0)),
            pl.BlockSpec((d, 4 * d), lambda i: (0, 0)),
            pl.BlockSpec((d, d), lambda i: (0, 0)),
        ],
        out_specs=pl.BlockSpec((tm, d), row),
        out_shape=jax.ShapeDtypeStruct((m, d), F32),
        compiler_params=_cparams(("parallel",)),
        name="merge_branches",
    )(x, h, *outs, wb, wg, wo)


def _split_w_in(w_in, b_fgt):
    wq = w_in[:, :D_MODEL].astype(BF16)
    wkT = w_in[:, D_MODEL:D_MODEL + KV_COLS].T.astype(BF16)
    wvT = w_in[:, D_MODEL + KV_COLS:D_MODEL + 2 * KV_COLS].T.astype(BF16)
    rest = w_in[:, D_MODEL + 2 * KV_COLS:]
    wm = jnp.pad(rest, ((0, 0), (0, MISC_COLS - rest.shape[1]))).astype(BF16)
    bias = jnp.zeros((1, 128), F32).at[0, FGT_LANE0:FGT_LANE0 + HEADS].set(b_fgt)
    return wq, wkT, wvT, wm, bias


def prompt_branches(x, g_mix, w_parts, cmp, *, batch, seq, tm):
    wq, wkT, wvT, wm, bias = w_parts
    h, q, kT, vT, kTb, vTb, misc, cum, kmT = mixer_project_prompt(x, g_mix, wq, wkT, wvT, wm, bias,
                                                                  tm=tm, batch=batch, seq=seq)
    B, S = batch, seq
    q3 = q.reshape(B, S, D_MODEL)
    misc3 = misc.reshape(B, S, MISC_COLS)
    cum3 = cum.reshape(B, S, 128)

    o_sb = sb_attention(q3, kTb, vTb)

    pos_k, w1k, w2k, pos_v, w1v, w2v = cmp
    n16 = S // CMP_STRIDE
    cw = CMP_STRIDE * HEAD_DIM
    cmp_rows = slice(CMP_KV * HEAD_DIM, (CMP_KV + 1) * HEAD_DIM)
    xs = jnp.stack([jnp.swapaxes(kT[:, cmp_rows, :], 1, 2).reshape(B, n16, cw),
                    jnp.swapaxes(vT[:, cmp_rows, :], 1, 2).reshape(B, n16, cw)])
    kvc = nsa_compress(xs, jnp.stack([pos_k, pos_v]), jnp.stack([w1k, w1v]), jnp.stack([w2k, w2v]))
    kcT = jnp.swapaxes(kvc[0], 1, 2).astype(BF16)
    vc = kvc[1].astype(BF16)
    kwT = jnp.swapaxes(misc3[:, :, 0:HEAD_DIM], 1, 2).astype(BF16)
    vwT = jnp.swapaxes(misc3[:, :, HEAD_DIM:2 * HEAD_DIM], 1, 2).astype(BF16)
    vwT = jnp.pad(vwT, ((0, 0), (0, HEAD_DIM), (0, 0)))
    o_nsa = nsa_attention(q3, kTb, vTb, kcT, vc, kwT, vwT, misc3, n16 - 1)

    o_moba = moba_attention(q3, kTb, vTb, kmT)

    cumT = jnp.swapaxes(cum3[:, :, FGT_LANE0:FGT_LANE0 + HEADS], 1, 2)
    cumT = jnp.pad(cumT, ((0, 0), (0, 8 - HEADS), (0, 0)))
    o_fox = fox_attention(q3, kTb, vTb, cum3, cumT)

    outs = [o.reshape(B * S, BRANCH_WIDTH) for o in (o_sb, o_nsa, o_moba, o_fox)]
    wb_len = min(WINDOW, S)
    to_heads = lambda aT: jnp.transpose(aT.reshape(B, N_KV_HEADS, HEAD_DIM, S), (0, 3, 1, 2))
    state = (to_heads(kT), to_heads(vT),
             misc3[:, :, 128 + FGT_LANE0:128 + FGT_LANE0 + HEADS],
             misc3[:, S - wb_len:, 0:HEAD_DIM].reshape(B, wb_len, 1, HEAD_DIM),
             misc3[:, S - wb_len:, HEAD_DIM:2 * HEAD_DIM].reshape(B, wb_len, 1, HEAD_DIM))
    return h, outs, state


def prompt_mixers(x, g_mix, w_parts, cmp, wb, wg, wo, *, batch, seq, tm):
    h, outs, state = prompt_branches(x, g_mix, w_parts, cmp, batch=batch, seq=seq, tm=tm)
    return merge_branches(x, h, outs, wb, wg, wo, tm=tm), state


PAGE = 128
N_PAGES = 16
PAST = PAGE * N_PAGES
Q_ROWS = 32
SUM_CHUNK = 256
G_SB, G_NSA, G_MOBA, G_FOX = 0, 1, 2, 3


def _row_const(vals):
    r = lax.broadcasted_iota(jnp.int32, (8, 1), 0)
    out = jnp.zeros((8, 1), F32)
    for i, v in enumerate(vals):
        out = jnp.where(r == i, v, out)
    return out


def _suffix_sum_excl(x, u, n_split):
    n = x.shape[1] // SUM_CHUNK
    carry = jnp.zeros((x.shape[0], 1), F32)
    outs = [None] * n
    for c in reversed(range(n)):
        xc = x[:, c * SUM_CHUNK:(c + 1) * SUM_CHUNK]
        parts = _split3(xc) if n_split == 3 else _split2(xc)
        loc = _dot(parts[0], u)
        for p in parts[1:]:
            loc = loc + _dot(p, u)
        outs[c] = loc + carry
        carry = carry + (loc[:, 0:1] + xc[:, 0:1])
    return jnp.concatenate(outs, axis=1)


def _decode_body(pt_ref, *refs):
    kp = refs[0:N_PAGES]
    vp = refs[N_PAGES:2 * N_PAGES]
    lfp = refs[2 * N_PAGES:3 * N_PAGES]
    (qm_ref, qn_ref, kn_ref, vn_ref, hs_ref, wn_ref, wkT_ref, wvT_ref, pos_ref, w1_ref, w2_ref,
     u_ref, cover_ref, es_ref, o_ref, s_scr, p_scr, cmp_scr) = refs[3 * N_PAGES:]
    del pt_ref
    qm = qm_ref[0]
    lane8 = lax.broadcasted_iota(jnp.int32, (8, 128), 1)
    row8 = lax.broadcasted_iota(jnp.int32, (8, 128), 0)
    live = row8[:, 0:1] < HEADS
    moba_rows = slice(MOBA_KV0 * HEAD_DIM, (MOBA_KV0 + HEADS) * HEAD_DIM)
    cmp_rows = slice(CMP_KV * HEAD_DIM, (CMP_KV + 2) * HEAD_DIM)

    ksum = []
    for j in range(N_PAGES):
        kT = kp[j][0, 0]
        s_scr[:, j * PAGE:(j + 1) * PAGE] = _dot(qm, kT.astype(BF16))
        ksum.append(jnp.sum(kT[moba_rows, :], axis=1, keepdims=True))
        cmp_scr[0, j * PAGE:(j + 1) * PAGE, :] = kT[cmp_rows, :].T
        cmp_scr[1, j * PAGE:(j + 1) * PAGE, :] = vp[j][0, 0, cmp_rows, :].T
    kn = kn_ref[0].astype(BF16).astype(F32)
    z_new = jnp.sum(qm.astype(F32) * kn, axis=1, keepdims=True)

    pos_f = lax.broadcasted_iota(jnp.int32, (8, PAST), 1).astype(F32)
    dist_f = float(PAST) - pos_f
    u = u_ref[...]
    hs = hs_ref[0]

    z = s_scr[8 * G_SB:8 * G_SB + 8, :]
    sp = _softplus(z)
    later = _suffix_sum_excl(-sp, u, 2)
    p_scr[8 * G_SB:8 * G_SB + 8, :] = jnp.exp(z - sp + later).astype(BF16)
    pnew = [jnp.zeros((8, 1), F32)]
    inv = [jnp.ones((8, 1), F32)]

    kvc = []
    for t in range(2):
        a = jnp.zeros((PAGE, CMP_HIDDEN), F32)
        b = jnp.zeros((PAGE, CMP_HIDDEN), F32)
        for r in range(CMP_STRIDE):
            xr = cmp_scr[t, pl.ds(r, PAST // CMP_STRIDE, stride=CMP_STRIDE), :][:, 0:HEAD_DIM]
            a = a + _dot((xr + pos_ref[t, r:r + 1, :]).astype(BF16), w1_ref[t, r])
            b = b + _dot((xr + pos_ref[t, CMP_STRIDE + r:CMP_STRIDE + r + 1, :]).astype(BF16),
                         w1_ref[t, CMP_STRIDE + r])
        hid = a + pltpu.roll(b, PAGE - 1, 0)
        kvc.append(_dot(_gelu_tanh(hid).astype(BF16), w2_ref[t]).astype(BF16))
    kc, vc = kvc
    n_cmp = PAST // CMP_STRIDE - 1

    qn = qn_ref[0]
    slope_n = _row_const(NSA_SLOPES)
    dist_c = float(PAST - (CMP_LEN - 1)) - float(CMP_STRIDE) * lane8.astype(F32)
    mask_c = lane8 < n_cmp
    s_c = jnp.where(mask_c, _dot_nt(qn, kc) - slope_n * dist_c, NEG)
    p_c = jnp.where(mask_c, jnp.exp(s_c - jnp.max(s_c, axis=1, keepdims=True)), 0.0)
    p_c = p_c / jnp.maximum(jnp.sum(p_c, axis=1, keepdims=True), 1e-30)
    p_cb = jnp.where(live, p_c, 0.0).astype(BF16)
    o_c = _dot(p_cb, vc)
    imp = jnp.sum(_dot(p_cb, cover_ref[...]), axis=0, keepdims=True)

    q_blk = PAST // SEL_BLOCK
    lane1 = lane8[0:1, :]
    forced = (lane1 == 0) | (lane1 > q_blk - N_LOCAL_SEL)
    score = jnp.where(lane1 <= q_blk, jnp.where(forced, -NEG, imp), NEG)
    sel = _rank_select(jnp.broadcast_to(score, (8, 128)), q_blk + 1, N_SEL)
    sel = jnp.where(lane8 <= q_blk, sel, 0.0)
    selm = _dot(sel.astype(BF16), es_ref[...]) > 0.5
    z = s_scr[8 * G_NSA:8 * G_NSA + 8, :]
    s_s = jnp.where(selm, z - slope_n * dist_f, NEG)
    zn = z_new[8 * G_NSA:8 * G_NSA + 8, :]
    m = jnp.maximum(jnp.max(s_s, axis=1, keepdims=True), zn)
    p = jnp.where(selm, jnp.exp(s_s - m), 0.0)
    pn = jnp.exp(zn - m)
    p_scr[8 * G_NSA:8 * G_NSA + 8, :] = p.astype(BF16)
    pnew.append(pn)
    inv.append(1.0 / jnp.maximum(jnp.sum(p, axis=1, keepdims=True) + pn, 1e-30))

    wl = lax.broadcasted_iota(jnp.int32, (8, WINDOW), 1)
    mask_w = wl >= 1
    wn = wn_ref[0]
    s_w = jnp.where(mask_w, _dot(qn, wkT_ref[0, 0].astype(BF16)) - slope_n * (float(WINDOW) - wl.astype(F32)), NEG)
    z_wn = jnp.sum(qn.astype(F32) * wn[:, 0:HEAD_DIM].astype(BF16).astype(F32), axis=1, keepdims=True)
    m = jnp.maximum(jnp.max(s_w, axis=1, keepdims=True), z_wn)
    p = jnp.where(mask_w, jnp.exp(s_w - m), 0.0)
    pn = jnp.exp(z_wn - m)
    o_w = (_dot_nt(p.astype(BF16), wvT_ref[0, 0].astype(BF16))
           + pn * wn[:, HEAD_DIM:2 * HEAD_DIM].astype(BF16).astype(F32))
    o_w = o_w / jnp.maximum(jnp.sum(p, axis=1, keepdims=True) + pn, 1e-30)

    n_blk = PAST // MOBA_BLOCK
    per_blk = MOBA_BLOCK // PAGE
    lane_km = lax.broadcasted_iota(jnp.int32, (BRANCH_WIDTH, 128), 1)
    kmT = jnp.zeros((BRANCH_WIDTH, 128), F32)
    for n in range(n_blk):
        kmT = jnp.where(lane_km == n, sum(ksum[n * per_blk:(n + 1) * per_blk]) * (1.0 / MOBA_BLOCK), kmT)
    gate = _dot(qm_ref[0, 8 * G_MOBA:8 * G_MOBA + 8, moba_rows], kmT.astype(BF16))
    gate = jnp.where(lane8 < n_blk, gate, NEG)
    selb = _rank_select(gate, n_blk, MOBA_TOPK)
    z = s_scr[8 * G_MOBA:8 * G_MOBA + 8, :]
    slope_m = _row_const(MOBA_SLOPES)
    sel_cols = jnp.concatenate([jnp.broadcast_to(selb[:, n:n + 1], (8, MOBA_BLOCK)) for n in range(n_blk)], axis=1)
    selk = sel_cols > 0.5
    s_m = jnp.where(selk, z - slope_m * dist_f, NEG)
    zn = z_new[8 * G_MOBA:8 * G_MOBA + 8, :]
    m = jnp.maximum(jnp.max(s_m, axis=1, keepdims=True), zn)
    p = jnp.where(selk, jnp.exp(s_m - m), 0.0)
    pn = jnp.exp(zn - m)
    p_scr[8 * G_MOBA:8 * G_MOBA + 8, :] = p.astype(BF16)
    pnew.append(pn)
    inv.append(1.0 / jnp.maximum(jnp.sum(p, axis=1, keepdims=True) + pn, 1e-30))

    lf = jnp.concatenate([pg[0, 0] for pg in lfp], axis=1)
    bias = _suffix_sum_excl(lf, u, 3) + hs[:, 3:4]
    z = s_scr[8 * G_FOX:8 * G_FOX + 8, :]
    s_f = z + bias
    zn = z_new[8 * G_FOX:8 * G_FOX + 8, :]
    m = jnp.maximum(jnp.max(s_f, axis=1, keepdims=True), zn)
    p = jnp.exp(s_f - m)
    pn = jnp.exp(zn - m)
    p_scr[8 * G_FOX:8 * G_FOX + 8, :] = p.astype(BF16)
    pnew.append(pn)
    inv.append(1.0 / jnp.maximum(jnp.sum(p, axis=1, keepdims=True) + pn, 1e-30))

    res = jnp.concatenate(pnew, axis=0) * vn_ref[0].astype(BF16).astype(F32)
    for j in range(N_PAGES):
        res = res + _dot_nt(p_scr[:, j * PAGE:(j + 1) * PAGE], vp[j][0, 0].astype(BF16))
    res = res * jnp.concatenate(inv, axis=0)

    def head_lanes(g, kv):
        return res[8 * g:8 * g + 8, kv * HEAD_DIM:(kv + 1) * HEAD_DIM]

    gsig = _sigmoid(hs)
    o_nsa = gsig[:, 0:1] * o_c + gsig[:, 1:2] * head_lanes(G_NSA, SEL_KV) + gsig[:, 2:3] * o_w
    pieces = [head_lanes(G_SB, h)[h:h + 1, :] for h in range(HEADS)]
    pieces += [o_nsa[h:h + 1, :] for h in range(HEADS)]
    pieces += [head_lanes(G_MOBA, MOBA_KV0 + h)[h:h + 1, :] for h in range(HEADS)]
    pieces += [head_lanes(G_FOX, FOX_KV0 + h)[h:h + 1, :] for h in range(HEADS)]
    o_ref[0] = jnp.concatenate(pieces, axis=1).astype(BF16)


def _query_rows(q):
    b = q.shape[0]
    out = jnp.zeros((b, Q_ROWS, KV_COLS), q.dtype)
    kv_of = {G_SB: lambda h: h, G_NSA: lambda h: SEL_KV, G_MOBA: lambda h: MOBA_KV0 + h, G_FOX: lambda h: FOX_KV0 + h}
    for g in range(4):
        for h in range(HEADS):
            kv = kv_of[g](h)
            qh = q[:, (g * HEADS + h) * HEAD_DIM:(g * HEADS + h + 1) * HEAD_DIM]
            out = out.at[:, 8 * g + h, kv * HEAD_DIM:(kv + 1) * HEAD_DIM].set(qh)
    return out


def decode_attention(q, k_new, v_new, misc, ckT, cvT, clf, wkT, wvT, page_table, cmp, layer):
    b = q.shape[0]
    pos_k, w1k, w2k, pos_v, w1v, w2v = cmp
    qm = _query_rows(q)
    qn = jnp.pad(q[:, BRANCH_WIDTH:2 * BRANCH_WIDTH].reshape(b, HEADS, HEAD_DIM), ((0, 0), (0, 8 - HEADS), (0, 0)))
    gates = misc[:, 128:128 + 3 * HEADS].reshape(b, HEADS, 3)
    lf_new = misc[:, 128 + FGT_LANE0:128 + FGT_LANE0 + HEADS].reshape(b, HEADS, 1)
    hs = jnp.pad(jnp.concatenate([gates, lf_new], axis=2), ((0, 0), (0, 8 - HEADS), (0, 124)))
    wn = misc[:, 0:128].reshape(b, 1, 128)
    pos = jnp.stack([pos_k, pos_v])
    w1 = jnp.stack([w1k, w1v]).reshape(2, CMP_LEN, HEAD_DIM, CMP_HIDDEN).astype(BF16)
    w2 = jnp.stack([w2k, w2v]).astype(BF16)

    def page_map(j):
        return lambda i, pt: (layer, pt[i, j], 0, 0)

    per_b3 = lambda i, pt: (i, 0, 0)
    per_b4 = lambda i, pt: (layer, i, 0, 0)
    const2 = lambda i, pt: (0, 0)
    const3 = lambda i, pt: (0, 0, 0)
    const4 = lambda i, pt: (0, 0, 0, 0)
    in_specs = ([pl.BlockSpec((1, 1, KV_COLS, PAGE), page_map(j)) for j in range(N_PAGES)]
                + [pl.BlockSpec((1, 1, KV_COLS, PAGE), page_map(j)) for j in range(N_PAGES)]
                + [pl.BlockSpec((1, 1, 8, PAGE), page_map(j)) for j in range(N_PAGES)]
                + [pl.BlockSpec((1, Q_ROWS, KV_COLS), per_b3),
                   pl.BlockSpec((1, 8, HEAD_DIM), per_b3),
                   pl.BlockSpec((1, 1, KV_COLS), per_b3),
                   pl.BlockSpec((1, 1, KV_COLS), per_b3),
                   pl.BlockSpec((1, 8, 128), per_b3),
                   pl.BlockSpec((1, 1, 128), per_b3),
                   pl.BlockSpec((1, 1, HEAD_DIM, WINDOW), per_b4),
                   pl.BlockSpec((1, 1, HEAD_DIM, WINDOW), per_b4),
                   pl.BlockSpec((2, CMP_LEN, HEAD_DIM), const3),
                   pl.BlockSpec((2, CMP_LEN, HEAD_DIM, CMP_HIDDEN), const4),
                   pl.BlockSpec((2, CMP_HIDDEN, HEAD_DIM), const3),
                   pl.BlockSpec((SUM_CHUNK, SUM_CHUNK), const2),
                   pl.BlockSpec((128, 128), const2),
                   pl.BlockSpec((128, PAST), const2)])
    grid_spec = pltpu.PrefetchScalarGridSpec(
        num_scalar_prefetch=1,
        grid=(b,),
        in_specs=in_specs,
        out_specs=pl.BlockSpec((1, 1, D_MODEL), per_b3),
        scratch_shapes=[pltpu.VMEM((Q_ROWS, PAST), F32), pltpu.VMEM((Q_ROWS, PAST), BF16),
                        pltpu.VMEM((2, PAST, 128), F32)],
    )
    out = pl.pallas_call(
        _decode_body,
        grid_spec=grid_spec,
        out_shape=jax.ShapeDtypeStruct((b, 1, D_MODEL), BF16),
        compiler_params=_cparams(("arbitrary",)),
        name="decode_attention",
    )(page_table, *([ckT] * N_PAGES), *([cvT] * N_PAGES), *([clf] * N_PAGES),
      qm, qn, k_new.reshape(b, 1, KV_COLS), v_new.reshape(b, 1, KV_COLS), hs, wn, wkT, wvT,
      pos, w1, w2, _strict_upper_sum_matrix(SUM_CHUNK), _cover_matrix()[:128], _block_expand_matrix(SEL_BLOCK, PAST))
    return out.reshape(b, D_MODEL)


def sample_mixers(x, g_mix, w_parts, cmp, wb, wg, wo, ckT, cvT, clf, wkT, wvT, wk_buf, wv_buf, page_table, layer):
    b = x.shape[0]
    wq, wkT_w, wvT_w, wm, bias = w_parts
    h, q, k, v, misc = mixer_project_sample(x, g_mix, wq, wkT_w, wvT_w, wm, bias)
    o = decode_attention(q, k, v, misc, ckT, cvT, clf, wkT, wvT, page_table, cmp, layer)
    outs = [o[:, i * BRANCH_WIDTH:(i + 1) * BRANCH_WIDTH] for i in range(4)]
    x = merge_branches(x, h, outs, wb, wg, wo, tm=b)
    kw = misc[:, 0:HEAD_DIM].reshape(b, 1, 1, HEAD_DIM)
    vw = misc[:, HEAD_DIM:2 * HEAD_DIM].reshape(b, 1, 1, HEAD_DIM)
    state = (k.reshape(b, 1, N_KV_HEADS, HEAD_DIM), v.reshape(b, 1, N_KV_HEADS, HEAD_DIM),
             misc[:, 128 + FGT_LANE0:128 + FGT_LANE0 + HEADS].reshape(b, 1, HEADS),
             jnp.concatenate([wk_buf[:, 1:], kw], axis=1),
             jnp.concatenate([wv_buf[:, 1:], vw], axis=1))
    return x, state


PROMPT_TM = 512


def kernel(x_prompt, x_sample, cache_k, cache_v, cache_logf, state_win_k, state_win_v, page_table,
           g_ffn1, w_ffn1_gate, w_ffn1_up, w_ffn1_down, g_mix, w_in, b_fgt,
           cmp_pos_k, cmp_w1_k, cmp_w2_k, cmp_pos_v, cmp_w1_v, cmp_w2_v,
           w_branch, w_merge_gate, w_out, g_ffn2, w_ffn2_gate, w_ffn2_up, w_ffn2_down, g_final):
    B, S, d = x_prompt.shape
    depth = g_ffn1.shape[0]
    nb = x_sample.shape[0]
    n_phys = cache_k.shape[1]
    xp = x_prompt.reshape(B * S, d)
    xs = x_sample.reshape(nb, d)
    pst = [[] for _ in range(5)]
    sst = [[] for _ in range(5)]
    ckT = jnp.transpose(cache_k, (0, 1, 3, 4, 2)).reshape(depth, n_phys, KV_COLS, PAGE)
    cvT = jnp.transpose(cache_v, (0, 1, 3, 4, 2)).reshape(depth, n_phys, KV_COLS, PAGE)
    clf = jnp.pad(jnp.swapaxes(cache_logf, 2, 3), ((0, 0), (0, 0), (0, 8 - HEADS), (0, 0)))
    wkT = jnp.transpose(state_win_k, (0, 1, 3, 4, 2)).reshape(depth, nb, HEAD_DIM, WINDOW)
    wvT = jnp.transpose(state_win_v, (0, 1, 3, 4, 2)).reshape(depth, nb, HEAD_DIM, WINDOW)
    for l in range(depth):
        last = l == depth - 1
        f1 = (w_ffn1_gate[l].astype(BF16), w_ffn1_up[l].astype(BF16), w_ffn1_down[l].astype(BF16))
        f2 = (w_ffn2_gate[l].astype(BF16), w_ffn2_up[l].astype(BF16), w_ffn2_down[l].astype(BF16))
        w_parts = _split_w_in(w_in[l], b_fgt[l])
        cmp = (cmp_pos_k[l], cmp_w1_k[l], cmp_w2_k[l], cmp_pos_v[l], cmp_w1_v[l], cmp_w2_v[l])
        wb, wg, wo = w_branch[l].astype(BF16), w_merge_gate[l].astype(BF16), w_out[l].astype(BF16)
        xp = ffn_half(xp, g_ffn1[l], *f1, tm=PROMPT_TM)
        xp, st_p = prompt_mixers(xp, g_mix[l], w_parts, cmp, wb, wg, wo, batch=B, seq=S, tm=PROMPT_TM)
        xp = ffn_half(xp, g_ffn2[l], *f2, g_final if last else None, tm=PROMPT_TM)
        xs = ffn_half(xs, g_ffn1[l], *f1, tm=nb)
        xs, st_s = sample_mixers(xs, g_mix[l], w_parts, cmp, wb, wg, wo, ckT, cvT, clf, wkT, wvT,
                                 state_win_k[l], state_win_v[l], page_table, l)
        xs = ffn_half(xs, g_ffn2[l], *f2, g_final if last else None, tm=nb)
        for i in range(5):
            pst[i].append(st_p[i])
            sst[i].append(st_s[i])
    y_prompt = xp.reshape(B, S, d)
    y_sample = xs.reshape(nb, 1, d)
    return (y_prompt, y_sample) + tuple(jnp.stack(p) for p in pst) + tuple(jnp.stack(p) for p in sst)
```

```python
import functools

import numpy as np
import jax
import jax.numpy as jnp
from jax import lax
from jax.experimental import pallas as pl
from jax.experimental.pallas import tpu as pltpu

F32 = jnp.float32
BF16 = jnp.bfloat16

D_MODEL = 1024
HEAD_DIM = 64
HEADS = 4
BRANCH_WIDTH = HEADS * HEAD_DIM
N_KV_HEADS = 14
KV_COLS = N_KV_HEADS * HEAD_DIM
CMP_KV = 4
SEL_KV = 5
MOBA_KV0 = 6
FOX_KV0 = 10
CMP_STRIDE = 16
CMP_LEN = 32
CMP_HIDDEN = 128
SEL_BLOCK = 64
N_SEL = 16
N_LOCAL_SEL = 2
WINDOW = 512
MOBA_BLOCK = 256
MOBA_TOPK = 3
D_FF = 2816
RMS_EPS = 1e-6
Q_SCALE = HEAD_DIM ** -0.5
NEG = -1e30
MISC_COLS = 256
FGT_LANE0 = 12
NSA_SLOPES = (2.0 ** -1, 2.0 ** -3, 2.0 ** -5, 2.0 ** -7)
MOBA_SLOPES = (2.0 ** -2, 2.0 ** -4, 2.0 ** -6, 2.0 ** -8)
SB_FLOOR = -110.0

VMEM_LIMIT = 56 * 1024 * 1024


def _cparams(sem):
    return pltpu.CompilerParams(dimension_semantics=sem, vmem_limit_bytes=VMEM_LIMIT)


def _dot(a, b):
    return jnp.dot(a, b, preferred_element_type=F32)


def _dot_nt(a, b):
    return lax.dot_general(a, b, (((1,), (1,)), ((), ())), preferred_element_type=F32)


def _rms(x, g):
    return x * lax.rsqrt(jnp.mean(x * x, axis=-1, keepdims=True) + RMS_EPS) * g


def _softplus(z):
    return jnp.maximum(z, 0.0) + jnp.log(1.0 + jnp.exp(-jnp.abs(z)))


def _sigmoid(z):
    return 1.0 / (1.0 + jnp.exp(-z))


def _split2(a):
    hi = a.astype(BF16)
    lo = (a - hi.astype(F32)).astype(BF16)
    return hi, lo


def _split3(a):
    a1 = a.astype(BF16)
    r1 = a - a1.astype(F32)
    a2 = r1.astype(BF16)
    a3 = (r1 - a2.astype(F32)).astype(BF16)
    return a1, a2, a3


def _ffn_body(x_ref, g_ref, wg_ref, wu_ref, wd_ref, gf_ref, o_ref, h_scr, acc_scr, *, n_f, final_norm):
    j = pl.program_id(1)

    @pl.when(j == 0)
    def _():
        h_scr[...] = _rms(x_ref[...], g_ref[...]).astype(BF16)
        acc_scr[...] = jnp.zeros_like(acc_scr)

    h = h_scr[...]
    a = _dot(h, wg_ref[...])
    u = _dot(h, wu_ref[...])
    act = (a * _sigmoid(a)) * u
    acc_scr[...] += _dot(act.astype(BF16), wd_ref[...])

    @pl.when(j == n_f - 1)
    def _():
        y = x_ref[...] + 0.5 * acc_scr[...]
        if final_norm:
            y = _rms(y, gf_ref[...])
        o_ref[...] = y


def ffn_half(x, g, wg, wu, wd, g_final=None, *, tm, tf=1408):
    m, d = x.shape
    f = wg.shape[1]
    n_f = f // tf
    final_norm = g_final is not None
    gf = g_final if final_norm else g
    body = functools.partial(_ffn_body, n_f=n_f, final_norm=final_norm)
    return pl.pallas_call(
        body,
        grid=(m // tm, n_f),
        in_specs=[
            pl.BlockSpec((tm, d), lambda i, j: (i, 0)),
            pl.BlockSpec((1, d), lambda i, j: (0, 0)),
            pl.BlockSpec((d, tf), lambda i, j: (0, j)),
            pl.BlockSpec((d, tf), lambda i, j: (0, j)),
            pl.BlockSpec((tf, d), lambda i, j: (j, 0)),
            pl.BlockSpec((1, d), lambda i, j: (0, 0)),
        ],
        out_specs=pl.BlockSpec((tm, d), lambda i, j: (i, 0)),
        out_shape=jax.ShapeDtypeStruct((m, d), F32),
        scratch_shapes=[pltpu.VMEM((tm, d), BF16), pltpu.VMEM((tm, d), F32)],
        compiler_params=_cparams(("parallel", "arbitrary")),
        name="ffn_half",
    )(x, g.reshape(1, d), wg, wu, wd, gf.reshape(1, d))


def _misc_and_logf(h, wm_ref, bias_ref, misc_ref):
    mz = _dot(h, wm_ref[...])
    grp = mz[:, 128:256]
    lane = lax.broadcasted_iota(jnp.int32, grp.shape, 1)
    is_f = (lane >= FGT_LANE0) & (lane < FGT_LANE0 + HEADS)
    xf = grp + bias_ref[...]
    lf = jnp.minimum(xf, 0.0) - jnp.log(1.0 + jnp.exp(-jnp.abs(xf)))
    lf = jnp.where(is_f, lf, 0.0)
    misc_ref[:, 0:128] = mz[:, 0:128]
    misc_ref[:, 128:256] = jnp.where(is_f, lf, grp)
    return lf


def _proj_prompt_body(x_ref, g_ref, wq_ref, wkT_ref, wvT_ref, wm_ref, bias_ref, tri_ref,
                      h_ref, q_ref, kT_ref, vT_ref, kTb_ref, vTb_ref, misc_ref, cum_ref, kmT_ref, carry_scr,
                      *, tm, tiles_per_seq):
    t = pl.program_id(0) % tiles_per_seq
    h = _rms(x_ref[...], g_ref[...]).astype(BF16)
    h_ref[...] = h
    q_ref[...] = (_dot(h, wq_ref[...]) * Q_SCALE).astype(BF16)
    kT = _dot_nt(wkT_ref[...], h)
    kT_ref[0] = kT
    kTb_ref[0] = kT.astype(BF16)
    vT = _dot_nt(wvT_ref[...], h)
    vT_ref[0] = vT
    vTb_ref[0] = vT.astype(BF16)
    lf = _misc_and_logf(h, wm_ref, bias_ref, misc_ref)

    @pl.when(t == 0)
    def _():
        carry_scr[...] = jnp.zeros_like(carry_scr)
        kmT_ref[...] = jnp.zeros_like(kmT_ref)

    a1, a2, a3 = _split3(lf)
    tri = tri_ref[...]
    cs = _dot(tri, a1) + _dot(tri, a2) + _dot(tri, a3) + carry_scr[0:1, :]
    cum_ref[...] = cs
    carry_scr[0:1, :] = cs[tm - 1:tm, :]

    lane = lax.broadcasted_iota(jnp.int32, (BRANCH_WIDTH, 128), 1)
    km = kmT_ref[0]
    for r in range(tm // MOBA_BLOCK):
        kblk = kT[MOBA_KV0 * HEAD_DIM:(MOBA_KV0 + HEADS) * HEAD_DIM, r * MOBA_BLOCK:(r + 1) * MOBA_BLOCK]
        col = jnp.sum(kblk, axis=1, keepdims=True) * (1.0 / MOBA_BLOCK)
        km = jnp.where(lane == t * (tm // MOBA_BLOCK) + r, col, km)
    kmT_ref[0] = km


def mixer_project_prompt(x, g, wq, wkT, wvT, wm, bias, *, tm, batch, seq):
    m, d = x.shape
    tps = seq // tm
    tri = jnp.asarray(np.tril(np.ones((tm, tm), np.float32)), BF16)
    row = lambda i: (i, 0)
    const = lambda i: (0, 0)
    featmaj = lambda i: (i // tps, 0, i % tps)
    kv_f32 = jax.ShapeDtypeStruct((batch, KV_COLS, seq), F32)
    kv_b16 = jax.ShapeDtypeStruct((batch, KV_COLS, seq), BF16)
    body = functools.partial(_proj_prompt_body, tm=tm, tiles_per_seq=tps)
    return pl.pallas_call(
        body,
        grid=(m // tm,),
        in_specs=[
            pl.BlockSpec((tm, d), row),
            pl.BlockSpec((1, d), const),
            pl.BlockSpec((d, d), const),
            pl.BlockSpec((KV_COLS, d), const),
            pl.BlockSpec((KV_COLS, d), const),
            pl.BlockSpec((d, MISC_COLS), const),
            pl.BlockSpec((1, 128), const),
            pl.BlockSpec((tm, tm), const),
        ],
        out_specs=[
            pl.BlockSpec((tm, d), row), pl.BlockSpec((tm, d), row),
            pl.BlockSpec((1, KV_COLS, tm), featmaj), pl.BlockSpec((1, KV_COLS, tm), featmaj),
            pl.BlockSpec((1, KV_COLS, tm), featmaj), pl.BlockSpec((1, KV_COLS, tm), featmaj),
            pl.BlockSpec((tm, MISC_COLS), row),
            pl.BlockSpec((tm, 128), row),
            pl.BlockSpec((1, BRANCH_WIDTH, 128), lambda i: (i // tps, 0, 0)),
        ],
        out_shape=[
            jax.ShapeDtypeStruct((m, d), BF16),
            jax.ShapeDtypeStruct((m, d), BF16),
            kv_f32, kv_f32, kv_b16, kv_b16,
            jax.ShapeDtypeStruct((m, MISC_COLS), F32),
            jax.ShapeDtypeStruct((m, 128), F32),
            jax.ShapeDtypeStruct((batch, BRANCH_WIDTH, 128), F32),
        ],
        scratch_shapes=[pltpu.VMEM((8, 128), F32)],
        compiler_params=_cparams(("arbitrary",)),
        name="mixer_project_prompt",
    )(x, g.reshape(1, d), wq, wkT, wvT, wm, bias, tri)


def _proj_sample_body(x_ref, g_ref, wq_ref, wkT_ref, wvT_ref, wm_ref, bias_ref,
                      h_ref, q_ref, k_ref, v_ref, misc_ref):
    h = _rms(x_ref[...], g_ref[...]).astype(BF16)
    h_ref[...] = h
    q_ref[...] = (_dot(h, wq_ref[...]) * Q_SCALE).astype(BF16)
    k_ref[...] = _dot_nt(h, wkT_ref[...])
    v_ref[...] = _dot_nt(h, wvT_ref[...])
    _misc_and_logf(h, wm_ref, bias_ref, misc_ref)


def mixer_project_sample(x, g, wq, wkT, wvT, wm, bias):
    m, d = x.shape
    row = lambda i: (0, 0)
    full = lambda shape: pl.BlockSpec(shape, row)
    return pl.pallas_call(
        _proj_sample_body,
        grid=(1,),
        in_specs=[full((m, d)), full((1, d)), full((d, d)), full((KV_COLS, d)), full((KV_COLS, d)),
                  full((d, MISC_COLS)), full((1, 128))],
        out_specs=[full((m, d)), full((m, d)), full((m, KV_COLS)), full((m, KV_COLS)), full((m, MISC_COLS))],
        out_shape=[
            jax.ShapeDtypeStruct((m, d), BF16),
            jax.ShapeDtypeStruct((m, d), BF16),
            jax.ShapeDtypeStruct((m, KV_COLS), F32),
            jax.ShapeDtypeStruct((m, KV_COLS), F32),
            jax.ShapeDtypeStruct((m, MISC_COLS), F32),
        ],
        compiler_params=_cparams(("arbitrary",)),
        name="mixer_project_sample",
    )(x, g.reshape(1, d), wq, wkT, wvT, wm, bias)


def _gelu_tanh(x):
    return 0.5 * x * (1.0 + jnp.tanh(np.sqrt(2.0 / np.pi).astype(np.float32) * (x + 0.044715 * (x * x * x))))


def _compress_body(xs_ref, pos_ref, w1a_ref, w1b_ref, w2_ref, o_ref):
    xs = xs_ref[0, 0]
    n16 = xs.shape[0]
    pos = pos_ref[0]
    a = _dot((xs + pos[0:1, :]).astype(BF16), w1a_ref[0])
    b = _dot((xs + pos[1:2, :]).astype(BF16), w1b_ref[0])
    hid = a + pltpu.roll(b, n16 - 1, 0)
    o_ref[0, 0] = _dot(_gelu_tanh(hid).astype(BF16), w2_ref[0])


def nsa_compress(xs, pos, w1, w2):
    _, b, n16, cw = xs.shape
    pos2 = pos.reshape(2, 2, cw)
    w1a = w1[:, :cw].astype(BF16)
    w1b = w1[:, cw:].astype(BF16)
    return pl.pallas_call(
        _compress_body,
        grid=(2, b),
        in_specs=[
            pl.BlockSpec((1, 1, n16, cw), lambda t, i: (t, i, 0, 0)),
            pl.BlockSpec((1, 2, cw), lambda t, i: (t, 0, 0)),
            pl.BlockSpec((1, cw, CMP_HIDDEN), lambda t, i: (t, 0, 0)),
            pl.BlockSpec((1, cw, CMP_HIDDEN), lambda t, i: (t, 0, 0)),
            pl.BlockSpec((1, CMP_HIDDEN, HEAD_DIM), lambda t, i: (t, 0, 0)),
        ],
        out_specs=pl.BlockSpec((1, 1, n16, HEAD_DIM), lambda t, i: (t, i, 0, 0)),
        out_shape=jax.ShapeDtypeStruct((2, b, n16, HEAD_DIM), F32),
        compiler_params=_cparams(("parallel", "parallel")),
        name="nsa_compress",
    )(xs, pos2, w1a, w1b, w2.astype(BF16))


TQ = 256
TK = 256


def _head_spec(s, head):
    return pl.BlockSpec((1, HEAD_DIM, s), lambda i, j, head=head: (i, head, 0))


def _pair_spec(s, pair):
    return pl.BlockSpec((1, 2 * HEAD_DIM, s), lambda i, j, pair=pair: (i, pair, 0))


def _merge_pairs(accs):
    lane = lax.broadcasted_iota(jnp.int32, accs[0].shape, 1)
    lo = lane < HEAD_DIM
    return jnp.concatenate([jnp.where(lo, accs[0], accs[1]), jnp.where(lo, accs[2], accs[3])], axis=1)


def _wide(x):
    return jnp.concatenate([x] * (TK // 128), axis=1)


def _online(s, mask, m, l, acc, vT):
    m_new = jnp.maximum(m, jnp.max(s, axis=1, keepdims=True))
    alpha = jnp.exp(m - m_new)
    p = jnp.exp(s - _wide(m_new))
    if mask is not None:
        p = jnp.where(mask, p, 0.0)
    l = alpha * l + jnp.sum(p, axis=1, keepdims=True)
    acc = alpha * acc + _dot_nt(p.astype(BF16), vT)
    return m_new, l, acc


def _softmax_init():
    return tuple((jnp.full((TQ, 128), NEG, F32), jnp.zeros((TQ, 128), F32), jnp.zeros((TQ, 128), F32))
                 for _ in range(HEADS))


def _finish(l, acc):
    return acc / jnp.maximum(l, 1e-30)


def _tile_iotas(qi):
    row = qi * TQ + lax.broadcasted_iota(jnp.int32, (TQ, TK), 0)
    lcol = lax.broadcasted_iota(jnp.int32, (TQ, TK), 1)
    return row, lcol


def _own_key_tile(qi):
    return (qi * TQ) // TK


def _sb_body(q_ref, k0_ref, k1_ref, k2_ref, k3_ref, v0_ref, v1_ref, u_ref, o_ref):
    qi = pl.program_id(1)
    k_refs = (k0_ref, k1_ref, k2_ref, k3_ref)
    v_refs = (v0_ref, v1_ref)
    row, lcol = _tile_iotas(qi)
    u = u_ref[...]
    qs = [q_ref[0, :, h * HEAD_DIM:(h + 1) * HEAD_DIM] for h in range(HEADS)]

    def tile(k0, before, accs, runs):
        new_a, new_r = [], []
        for h in range(HEADS):
            z = _dot(qs[h], k_refs[h][0, :, pl.ds(k0, TK)])
            sp = _softplus(z)
            lk = -sp if before is None else jnp.where(before, -sp, 0.0)
            hi, lo = _split2(lk)
            sums = _dot(hi, u) + _dot(lo, u)
            loc = sums[:, 0:TK]
            w = jnp.exp(z - sp + loc + _wide(runs[h]))
            if before is not None:
                w = jnp.where(before, w, 0.0)
            new_a.append(accs[h] + _dot_nt(w.astype(BF16), v_refs[h // 2][0, :, pl.ds(k0, TK)]))
            new_r.append(runs[h] + sums[:, TK:TK + 128])
        return tuple(new_a), tuple(new_r)

    zero_a = tuple(jnp.zeros((TQ, 128), F32) for _ in range(HEADS))
    zero_r = tuple(jnp.zeros((TQ, 128), F32) for _ in range(HEADS))
    kd = _own_key_tile(qi)
    q0 = pl.multiple_of(kd * TK, TK)
    accs, runs = tile(q0, (q0 + lcol) < row, zero_a, zero_r)

    def cond(c):
        jj, _, runs = c
        top = jnp.max(jnp.maximum(jnp.maximum(runs[0], runs[1]), jnp.maximum(runs[2], runs[3])))
        return jnp.logical_and(jj <= kd, top > SB_FLOOR)

    def body(c):
        jj, accs, runs = c
        accs, runs = tile(pl.multiple_of((kd - jj) * TK, TK), None, accs, runs)
        return jj + 1, accs, runs

    _, accs, _ = lax.while_loop(cond, body, (jnp.int32(1), accs, runs))
    o_ref[0] = _merge_pairs(accs).astype(BF16)


def _strict_upper_sum_matrix(n, total_cols=0):
    u = np.concatenate([np.tril(np.ones((n, n), np.float32), -1), np.ones((n, total_cols), np.float32)], axis=1)
    return jnp.asarray(u, BF16)


def sb_attention(q, kTb, vTb):
    b, s, _ = q.shape
    return pl.pallas_call(
        _sb_body,
        grid=(b, s // TQ),
        in_specs=[pl.BlockSpec((1, TQ, BRANCH_WIDTH), lambda i, j: (i, j, 0))]
        + [_head_spec(s, h) for h in range(HEADS)]
        + [_pair_spec(s, 0), _pair_spec(s, 1)]
        + [pl.BlockSpec((TK, TK + 128), lambda i, j: (0, 0))],
        out_specs=pl.BlockSpec((1, TQ, BRANCH_WIDTH), lambda i, j: (i, j, 0)),
        out_shape=jax.ShapeDtypeStruct((b, s, BRANCH_WIDTH), BF16),
        compiler_params=_cparams(("parallel", "arbitrary")),
        name="sb_attention",
    )(q, kTb, kTb, kTb, kTb, vTb, vTb, _strict_upper_sum_matrix(TK, 128))


def _fox_body(q_ref, k0_ref, k1_ref, k2_ref, k3_ref, v0_ref, v1_ref, cq_ref, ck_ref, o_ref):
    qi = pl.program_id(1)
    k_refs = (k0_ref, k1_ref, k2_ref, k3_ref)
    v_refs = (v0_ref, v1_ref)
    row, lcol = _tile_iotas(qi)
    qs = [q_ref[0, :, h * HEAD_DIM:(h + 1) * HEAD_DIM] for h in range(HEADS)]
    cqs = [jnp.broadcast_to(cq_ref[0, :, FGT_LANE0 + h:FGT_LANE0 + h + 1], (TQ, 128)) for h in range(HEADS)]

    def tile(k0, mask, carry):
        out = []
        for h in range(HEADS):
            m, l, acc = carry[h]
            z = _dot(qs[h], k_refs[h][0, :, pl.ds(k0, TK)])
            s = z + _wide(cqs[h]) - ck_ref[0, h:h + 1, pl.ds(k0, TK)]
            if mask is not None:
                s = jnp.where(mask, s, NEG)
            out.append(_online(s, mask, m, l, acc, v_refs[h // 2][0, :, pl.ds(k0, TK)]))
        return tuple(out)

    kd = _own_key_tile(qi)
    carry = lax.fori_loop(0, kd, lambda j, c: tile(pl.multiple_of(j * TK, TK), None, c), _softmax_init())
    q0 = pl.multiple_of(kd * TK, TK)
    carry = tile(q0, (q0 + lcol) <= row, carry)
    o_ref[0] = _merge_pairs([_finish(l, acc) for (_, l, acc) in carry]).astype(BF16)


def fox_attention(q, kTb, vTb, cum, cumT):
    b, s, _ = q.shape
    return pl.pallas_call(
        _fox_body,
        grid=(b, s // TQ),
        in_specs=[pl.BlockSpec((1, TQ, BRANCH_WIDTH), lambda i, j: (i, j, 3))]
        + [_head_spec(s, FOX_KV0 + h) for h in range(HEADS)]
        + [_pair_spec(s, FOX_KV0 // 2), _pair_spec(s, FOX_KV0 // 2 + 1)]
        + [pl.BlockSpec((1, TQ, 128), lambda i, j: (i, j, 0)),
           pl.BlockSpec((1, 8, s), lambda i, j: (i, 0, 0))],
        out_specs=pl.BlockSpec((1, TQ, BRANCH_WIDTH), lambda i, j: (i, j, 0)),
        out_shape=jax.ShapeDtypeStruct((b, s, BRANCH_WIDTH), BF16),
        compiler_params=_cparams(("parallel", "arbitrary")),
        name="fox_attention",
    )(q, kTb, kTb, kTb, kTb, vTb, vTb, cum, cumT)


def _rank_select(score, n_cols, k):
    lane = lax.broadcasted_iota(jnp.int32, score.shape, 1)
    rank = jnp.zeros(score.shape, F32)
    for c in range(n_cols):
        col = score[:, c:c + 1]
        ahead = jnp.where(col > score, 1.0, jnp.where(col == score, jnp.where(lane > c, 1.0, 0.0), 0.0))
        rank = rank + ahead
    return jnp.where(rank < k, 1.0, 0.0)


def _rank_select_rows(score, n_cols, k):
    n_pad = -(-n_cols // 8) * 8
    st = score.T[0:n_pad, :]
    cand = lax.broadcasted_iota(jnp.int32, st.shape, 0)
    rank = jnp.zeros(st.shape, F32)
    for c in range(n_cols):
        ref = st[c:c + 1, :]
        ahead = jnp.where(ref > st, 1.0, jnp.where(ref == st, jnp.where(cand > c, 1.0, 0.0), 0.0))
        rank = rank + ahead
    sel = jnp.where(rank < k, 1.0, 0.0)
    if n_pad < 128:
        sel = jnp.concatenate([sel, jnp.zeros((128 - n_pad, st.shape[1]), F32)], axis=0)
    return sel.T


def _moba_body(q_ref, k0_ref, k1_ref, k2_ref, k3_ref, v0_ref, v1_ref, kmT_ref, em_ref, o_ref):
    qi = pl.program_id(1)
    k_refs = (k0_ref, k1_ref, k2_ref, k3_ref)
    v_refs = (v0_ref, v1_ref)
    row, lcol = _tile_iotas(qi)
    lane = lax.broadcasted_iota(jnp.int32, (TQ, 128), 1)
    n_blocks = k0_ref.shape[2] // MOBA_BLOCK
    kd = _own_key_tile(qi)
    qs = [q_ref[0, :, h * HEAD_DIM:(h + 1) * HEAD_DIM] for h in range(HEADS)]
    sels = []
    for h in range(HEADS):
        gate = _dot(qs[h], kmT_ref[0, h * HEAD_DIM:(h + 1) * HEAD_DIM, :].astype(BF16))
        gate = jnp.where(lane < kd, gate, NEG)
        sel = _rank_select_rows(gate, n_blocks, MOBA_TOPK)
        sels.append(jnp.where(lane < kd, sel, 0.0).astype(BF16))

    def past_tile(j, carry):
        k0 = pl.multiple_of(j * TK, TK)
        distf = (row - (k0 + lcol)).astype(F32)
        out = []
        for h in range(HEADS):
            m, l, acc = carry[h]
            z = _dot(qs[h], k_refs[h][0, :, pl.ds(k0, TK)])
            bit = _dot(sels[h], em_ref[:, pl.ds(pl.multiple_of(j * 128, 128), 128)])
            s = z - MOBA_SLOPES[h] * distf
            m_new = jnp.maximum(m, jnp.where(bit > 0.5, jnp.max(s, axis=1, keepdims=True), NEG))
            alpha = jnp.exp(m - m_new)
            p = jnp.exp(jnp.minimum(s - _wide(m_new), 0.0))
            l = alpha * l + bit * jnp.sum(p, axis=1, keepdims=True)
            acc = alpha * acc + bit * _dot_nt(p.astype(BF16), v_refs[h // 2][0, :, pl.ds(k0, TK)])
            out.append((m_new, l, acc))
        return tuple(out)

    def own_tile(carry):
        k0 = pl.multiple_of(kd * TK, TK)
        dist = row - (k0 + lcol)
        mask = dist >= 0
        distf = dist.astype(F32)
        out = []
        for h in range(HEADS):
            m, l, acc = carry[h]
            z = _dot(qs[h], k_refs[h][0, :, pl.ds(k0, TK)])
            s = jnp.where(mask, z - MOBA_SLOPES[h] * distf, NEG)
            out.append(_online(s, mask, m, l, acc, v_refs[h // 2][0, :, pl.ds(k0, TK)]))
        return tuple(out)

    carry = own_tile(lax.fori_loop(0, kd, past_tile, _softmax_init()))
    o_ref[0] = _merge_pairs([_finish(l, acc) for (_, l, acc) in carry]).astype(BF16)


def _block_expand_matrix(block, s):
    e = (np.arange(128)[:, None] == (np.arange(s)[None, :] // block)).astype(np.float32)
    return jnp.asarray(e, BF16)


def moba_attention(q, kTb, vTb, kmT):
    b, s, _ = q.shape
    n_blocks = s // MOBA_BLOCK
    return pl.pallas_call(
        _moba_body,
        grid=(b, s // TQ),
        in_specs=[pl.BlockSpec((1, TQ, BRANCH_WIDTH), lambda i, j: (i, j, 2))]
        + [_head_spec(s, MOBA_KV0 + h) for h in range(HEADS)]
        + [_pair_spec(s, MOBA_KV0 // 2), _pair_spec(s, MOBA_KV0 // 2 + 1)]
        + [pl.BlockSpec((1, BRANCH_WIDTH, 128), lambda i, j: (i, 0, 0)),
           pl.BlockSpec((128, n_blocks * 128), lambda i, j: (0, 0))],
        out_specs=pl.BlockSpec((1, TQ, BRANCH_WIDTH), lambda i, j: (i, j, 0)),
        out_shape=jax.ShapeDtypeStruct((b, s, BRANCH_WIDTH), BF16),
        compiler_params=_cparams(("parallel", "arbitrary")),
        name="moba_attention",
    )(q, kTb, kTb, kTb, kTb, vTb, vTb, kmT, _block_expand_matrix(128, n_blocks * 128))


def _nsa_body(q_ref, kT_ref, vT_ref, kcT_ref, vc_ref, kwT_ref, vwT_ref, g_ref, cover_ref, es_ref, o_ref, st_scr,
              *, n_cmp):
    qi = pl.program_id(1)
    s_len = kT_ref.shape[2]
    n_sel_blocks = s_len // SEL_BLOCK
    row, lcol = _tile_iotas(qi)
    qs = [q_ref[0, :, h * HEAD_DIM:(h + 1) * HEAD_DIM] for h in range(HEADS)]

    ncp = kcT_ref.shape[2]
    crow = qi * TQ + lax.broadcasted_iota(jnp.int32, (TQ, ncp), 0)
    cn = lax.broadcasted_iota(jnp.int32, (TQ, ncp), 1)
    dist_c = crow - (cn * CMP_STRIDE + (CMP_LEN - 1))
    mask_c = jnp.where(cn < n_cmp, dist_c, -1) >= 0
    dist_cf = dist_c.astype(F32)
    kcT = kcT_ref[0]
    vc = vc_ref[0]
    cover = cover_ref[...]
    o_c = []
    imp = jnp.zeros((TQ, 128), F32)
    for h in range(HEADS):
        s = jnp.where(mask_c, _dot(qs[h], kcT) - NSA_SLOPES[h] * dist_cf, NEG)
        m = jnp.max(s, axis=1, keepdims=True)
        p = jnp.where(mask_c, jnp.exp(s - m), 0.0)
        p = p / jnp.maximum(jnp.sum(p, axis=1, keepdims=True), 1e-30)
        pb = p.astype(BF16)
        o_c.append(_dot(pb, vc))
        imp = imp + _dot(pb, cover)

    lane = lax.broadcasted_iota(jnp.int32, (TQ, 128), 1)
    q_blk = (qi * TQ + lax.broadcasted_iota(jnp.int32, (TQ, 128), 0)) // SEL_BLOCK
    forced = (lane == 0) | (lane > q_blk - N_LOCAL_SEL)
    score = jnp.where(lane <= q_blk, jnp.where(forced, -NEG, imp), NEG)
    sel_f = jnp.where(lane <= q_blk, _rank_select_rows(score, n_sel_blocks, N_SEL), 0.0)
    sel = sel_f.astype(BF16)
    picked = jnp.max(sel_f, axis=0, keepdims=True)
    lane1 = lane[0:1, :]

    for h in range(HEADS):
        st_scr[0, h] = jnp.full((TQ, 128), NEG, F32)
        st_scr[1, h] = jnp.zeros((TQ, 128), F32)
        st_scr[2, h] = jnp.zeros((TQ, 128), F32)

    def sel_tile(k0, diag):
        kT = kT_ref[0, :, pl.ds(k0, TK)]
        vT = vT_ref[0, :, pl.ds(k0, TK)]
        dist = row - (k0 + lcol)
        selm = _dot(sel, es_ref[:, pl.ds(k0, TK)])
        if diag:
            mask = jnp.where(dist >= 0, selm, 0.0) > 0.5
        else:
            mask = selm > 0.5
        distf = dist.astype(F32)
        for h in range(HEADS):
            s = jnp.where(mask, _dot(qs[h], kT) - NSA_SLOPES[h] * distf, NEG)
            m, l, acc = _online(s, mask, st_scr[0, h], st_scr[1, h], st_scr[2, h], vT)
            st_scr[0, h] = m
            st_scr[1, h] = l
            st_scr[2, h] = acc

    def past_tile(j, _):
        blk0 = j * (TK // SEL_BLOCK)
        in_tile = (lane1 >= blk0) & (lane1 < blk0 + TK // SEL_BLOCK)
        hit = jnp.max(jnp.where(in_tile, picked, 0.0))

        @pl.when(hit > 0.5)
        def _():
            sel_tile(pl.multiple_of(j * TK, TK), False)

        return 0

    kd = _own_key_tile(qi)
    lax.fori_loop(0, kd, past_tile, 0)
    sel_tile(pl.multiple_of(kd * TK, TK), True)
    o_s = [_finish(st_scr[1, h], st_scr[2, h]) for h in range(HEADS)]

    def win_body(j, carry):
        k0 = pl.multiple_of(j * TK, TK)
        kT = kwT_ref[0, :, pl.ds(k0, TK)]
        vT = vwT_ref[0, :, pl.ds(k0, TK)]
        dist = row - (k0 + lcol)
        mask = jnp.where(dist >= 0, dist, WINDOW) < WINDOW
        distf = dist.astype(F32)
        out = []
        for h in range(HEADS):
            m, l, acc = carry[h]
            s = jnp.where(mask, _dot(qs[h], kT) - NSA_SLOPES[h] * distf, NEG)
            out.append(_online(s, mask, m, l, acc, vT))
        return tuple(out)

    res_w = lax.fori_loop(jnp.maximum(kd - WINDOW // TK, 0), kd + 1, win_body, _softmax_init())
    o_w = [_finish(l, acc)[:, 0:HEAD_DIM] for (_, l, acc) in res_w]

    g = _sigmoid(g_ref[0])
    outs = []
    for h in range(HEADS):
        g0 = g[:, 3 * h + 0:3 * h + 1]
        g1 = g[:, 3 * h + 1:3 * h + 2]
        g2 = g[:, 3 * h + 2:3 * h + 3]
        outs.append(g0 * o_c[h] + g1 * o_s[h][:, HEAD_DIM:2 * HEAD_DIM] + g2 * o_w[h])
    o_ref[0] = jnp.concatenate(outs, axis=1).astype(BF16)


def _cover_matrix():
    n = np.arange(256)[:, None] * CMP_STRIDE
    m = np.arange(128)[None, :] * SEL_BLOCK
    return jnp.asarray(((n < m + SEL_BLOCK) & (n + CMP_LEN > m)).astype(np.float32), BF16)


def nsa_attention(q, kTb, vTb, kcT, vc, kwT, vwT, misc, n_cmp):
    b, s, _ = q.shape
    ncp = kcT.shape[2]
    body = functools.partial(_nsa_body, n_cmp=n_cmp)
    return pl.pallas_call(
        body,
        grid=(b, s // TQ),
        in_specs=[
            pl.BlockSpec((1, TQ, BRANCH_WIDTH), lambda i, j: (i, j, 1)),
            _head_spec(s, SEL_KV),
            _pair_spec(s, SEL_KV // 2),
            pl.BlockSpec((1, HEAD_DIM, ncp), lambda i, j: (i, 0, 0)),
            pl.BlockSpec((1, ncp, HEAD_DIM), lambda i, j: (i, 0, 0)),
            pl.BlockSpec((1, HEAD_DIM, s), lambda i, j: (i, 0, 0)),
            pl.BlockSpec((1, 2 * HEAD_DIM, s), lambda i, j: (i, 0, 0)),
            pl.BlockSpec((1, TQ, 128), lambda i, j: (i, j, 1)),
            pl.BlockSpec((ncp, 128), lambda i, j: (0, 0)),
            pl.BlockSpec((128, s), lambda i, j: (0, 0)),
        ],
        out_specs=pl.BlockSpec((1, TQ, BRANCH_WIDTH), lambda i, j: (i, j, 0)),
        out_shape=jax.ShapeDtypeStruct((b, s, BRANCH_WIDTH), BF16),
        scratch_shapes=[pltpu.VMEM((3, HEADS, TQ, 128), F32)],
        compiler_params=_cparams(("parallel", "arbitrary")),
        name="nsa_attention",
    )(q, kTb, vTb, kcT, vc, kwT, vwT, misc, _cover_matrix()[:ncp], _block_expand_matrix(SEL_BLOCK, s))


def _merge_body(x_ref, h_ref, o0_ref, o1_ref, o2_ref, o3_ref, wb_ref, wg_ref, wo_ref, out_ref):
    h = h_ref[...]
    d = x_ref.shape[1]
    mix = None
    for n, o_ref in enumerate((o0_ref, o1_ref, o2_ref, o3_ref)):
        gate = _sigmoid(_dot(h, wg_ref[:, n * d:(n + 1) * d]))
        term = gate * _dot(o_ref[...], wb_ref[n])
        mix = term if mix is None else mix + term
    out_ref[...] = x_ref[...] + _dot(mix.astype(BF16), wo_ref[...])


def merge_branches(x, h, outs, wb, wg, wo, *, tm):
    m, d = x.shape
    row = lambda i: (i, 0)
    return pl.pallas_call(
        _merge_body,
        grid=(m // tm,),
        in_specs=[
            pl.BlockSpec((tm, d), row),
            pl.BlockSpec((tm, d), row),
            pl.BlockSpec((tm, BRANCH_WIDTH), row),
            pl.BlockSpec((tm, BRANCH_WIDTH), row),
            pl.BlockSpec((tm, BRANCH_WIDTH), row),
            pl.BlockSpec((tm, BRANCH_WIDTH), row),
            pl.BlockSpec((4, BRANCH_WIDTH, d), lambda i: (0, 0, 0)),
            pl.BlockSpec((d, 4 * d), lambda i: (0, 0)),
            pl.BlockSpec((d, d), lambda i: (0, 0)),
        ],
        out_specs=pl.BlockSpec((tm, d), row),
        out_shape=jax.ShapeDtypeStruct((m, d), F32),
        compiler_params=_cparams(("parallel",)),
        name="merge_branches",
    )(x, h, *outs, wb, wg, wo)


def _split_w_in(w_in, b_fgt):
    wq = w_in[:, :D_MODEL].astype(BF16)
    wkT = w_in[:, D_MODEL:D_MODEL + KV_COLS].T.astype(BF16)
    wvT = w_in[:, D_MODEL + KV_COLS:D_MODEL + 2 * KV_COLS].T.astype(BF16)
    rest = w_in[:, D_MODEL + 2 * KV_COLS:]
    wm = jnp.pad(rest, ((0, 0), (0, MISC_COLS - rest.shape[1]))).astype(BF16)
    bias = jnp.zeros((1, 128), F32).at[0, FGT_LANE0:FGT_LANE0 + HEADS].set(b_fgt)
    return wq, wkT, wvT, wm, bias


def prompt_branches(x, g_mix, w_parts, cmp, *, batch, seq, tm):
    wq, wkT, wvT, wm, bias = w_parts
    h, q, kT, vT, kTb, vTb, misc, cum, kmT = mixer_project_prompt(x, g_mix, wq, wkT, wvT, wm, bias,
                                                                  tm=tm, batch=batch, seq=seq)
    B, S = batch, seq
    q3 = q.reshape(B, S, D_MODEL)
    misc3 = misc.reshape(B, S, MISC_COLS)
    cum3 = cum.reshape(B, S, 128)

    o_sb = sb_attention(q3, kTb, vTb)

    pos_k, w1k, w2k, pos_v, w1v, w2v = cmp
    n16 = S // CMP_STRIDE
    cw = CMP_STRIDE * HEAD_DIM
    cmp_rows = slice(CMP_KV * HEAD_DIM, (CMP_KV + 1) * HEAD_DIM)
    xs = jnp.stack([jnp.swapaxes(kT[:, cmp_rows, :], 1, 2).reshape(B, n16, cw),
                    jnp.swapaxes(vT[:, cmp_rows, :], 1, 2).reshape(B, n16, cw)])
    kvc = nsa_compress(xs, jnp.stack([pos_k, pos_v]), jnp.stack([w1k, w1v]), jnp.stack([w2k, w2v]))
    kcT = jnp.swapaxes(kvc[0], 1, 2).astype(BF16)
    vc = kvc[1].astype(BF16)
    kwT = jnp.swapaxes(misc3[:, :, 0:HEAD_DIM], 1, 2).astype(BF16)
    vwT = jnp.swapaxes(misc3[:, :, HEAD_DIM:2 * HEAD_DIM], 1, 2).astype(BF16)
    vwT = jnp.pad(vwT, ((0, 0), (0, HEAD_DIM), (0, 0)))
    o_nsa = nsa_attention(q3, kTb, vTb, kcT, vc, kwT, vwT, misc3, n16 - 1)

    o_moba = moba_attention(q3, kTb, vTb, kmT)

    cumT = jnp.swapaxes(cum3[:, :, FGT_LANE0:FGT_LANE0 + HEADS], 1, 2)
    cumT = jnp.pad(cumT, ((0, 0), (0, 8 - HEADS), (0, 0)))
    o_fox = fox_attention(q3, kTb, vTb, cum3, cumT)

    outs = [o.reshape(B * S, BRANCH_WIDTH) for o in (o_sb, o_nsa, o_moba, o_fox)]
    wb_len = min(WINDOW, S)
    to_heads = lambda aT: jnp.transpose(aT.reshape(B, N_KV_HEADS, HEAD_DIM, S), (0, 3, 1, 2))
    state = (to_heads(kT), to_heads(vT),
             misc3[:, :, 128 + FGT_LANE0:128 + FGT_LANE0 + HEADS],
             misc3[:, S - wb_len:, 0:HEAD_DIM].reshape(B, wb_len, 1, HEAD_DIM),
             misc3[:, S - wb_len:, HEAD_DIM:2 * HEAD_DIM].reshape(B, wb_len, 1, HEAD_DIM))
    return h, outs, state


def prompt_mixers(x, g_mix, w_parts, cmp, wb, wg, wo, *, batch, seq, tm):
    h, outs, state = prompt_branches(x, g_mix, w_parts, cmp, batch=batch, seq=seq, tm=tm)
    return merge_branches(x, h, outs, wb, wg, wo, tm=tm), state


PAGE = 128
N_PAGES = 16
PAST = PAGE * N_PAGES
Q_ROWS = 32
SUM_CHUNK = 256
G_SB, G_NSA, G_MOBA, G_FOX = 0, 1, 2, 3


def _row_const(vals):
    r = lax.broadcasted_iota(jnp.int32, (8, 1), 0)
    out = jnp.zeros((8, 1), F32)
    for i, v in enumerate(vals):
        out = jnp.where(r == i, v, out)
    return out


def _suffix_sum_excl(x, u, n_split):
    n = x.shape[1] // SUM_CHUNK
    carry = jnp.zeros((x.shape[0], 1), F32)
    outs = [None] * n
    for c in reversed(range(n)):
        xc = x[:, c * SUM_CHUNK:(c + 1) * SUM_CHUNK]
        parts = _split3(xc) if n_split == 3 else _split2(xc)
        loc = _dot(parts[0], u)
        for p in parts[1:]:
            loc = loc + _dot(p, u)
        outs[c] = loc + carry
        carry = carry + (loc[:, 0:1] + xc[:, 0:1])
    return jnp.concatenate(outs, axis=1)


def _decode_body(pt_ref, *refs):
    kp = refs[0:N_PAGES]
    vp = refs[N_PAGES:2 * N_PAGES]
    lfp = refs[2 * N_PAGES:3 * N_PAGES]
    (qm_ref, qn_ref, kn_ref, vn_ref, hs_ref, wn_ref, wkT_ref, wvT_ref, pos_ref, w1_ref, w2_ref,
     u_ref, cover_ref, es_ref, o_ref, s_scr, p_scr, cmp_scr) = refs[3 * N_PAGES:]
    del pt_ref
    qm = qm_ref[0]
    lane8 = lax.broadcasted_iota(jnp.int32, (8, 128), 1)
    row8 = lax.broadcasted_iota(jnp.int32, (8, 128), 0)
    live = row8[:, 0:1] < HEADS
    moba_rows = slice(MOBA_KV0 * HEAD_DIM, (MOBA_KV0 + HEADS) * HEAD_DIM)
    cmp_rows = slice(CMP_KV * HEAD_DIM, (CMP_KV + 2) * HEAD_DIM)

    ksum = []
    for j in range(N_PAGES):
        kT = kp[j][0, 0]
        s_scr[:, j * PAGE:(j + 1) * PAGE] = _dot(qm, kT.astype(BF16))
        ksum.append(jnp.sum(kT[moba_rows, :], axis=1, keepdims=True))
        cmp_scr[0, j * PAGE:(j + 1) * PAGE, :] = kT[cmp_rows, :].T
        cmp_scr[1, j * PAGE:(j + 1) * PAGE, :] = vp[j][0, 0, cmp_rows, :].T
    kn = kn_ref[0].astype(BF16).astype(F32)
    z_new = jnp.sum(qm.astype(F32) * kn, axis=1, keepdims=True)

    pos_f = lax.broadcasted_iota(jnp.int32, (8, PAST), 1).astype(F32)
    dist_f = float(PAST) - pos_f
    u = u_ref[...]
    hs = hs_ref[0]

    z = s_scr[8 * G_SB:8 * G_SB + 8, :]
    sp = _softplus(z)
    later = _suffix_sum_excl(-sp, u, 2)
    p_scr[8 * G_SB:8 * G_SB + 8, :] = jnp.exp(z - sp + later).astype(BF16)
    pnew = [jnp.zeros((8, 1), F32)]
    inv = [jnp.ones((8, 1), F32)]

    kvc = []
    for t in range(2):
        a = jnp.zeros((PAGE, CMP_HIDDEN), F32)
        b = jnp.zeros((PAGE, CMP_HIDDEN), F32)
        for r in range(CMP_STRIDE):
            xr = cmp_scr[t, pl.ds(r, PAST // CMP_STRIDE, stride=CMP_STRIDE), :][:, 0:HEAD_DIM]
            a = a + _dot((xr + pos_ref[t, r:r + 1, :]).astype(BF16), w1_ref[t, r])
            b = b + _dot((xr + pos_ref[t, CMP_STRIDE + r:CMP_STRIDE + r + 1, :]).astype(BF16),
                         w1_ref[t, CMP_STRIDE + r])
        hid = a + pltpu.roll(b, PAGE - 1, 0)
        kvc.append(_dot(_gelu_tanh(hid).astype(BF16), w2_ref[t]).astype(BF16))
    kc, vc = kvc
    n_cmp = PAST // CMP_STRIDE - 1

    qn = qn_ref[0]
    slope_n = _row_const(NSA_SLOPES)
    dist_c = float(PAST - (CMP_LEN - 1)) - float(CMP_STRIDE) * lane8.astype(F32)
    mask_c = lane8 < n_cmp
    s_c = jnp.where(mask_c, _dot_nt(qn, kc) - slope_n * dist_c, NEG)
    p_c = jnp.where(mask_c, jnp.exp(s_c - jnp.max(s_c, axis=1, keepdims=True)), 0.0)
    p_c = p_c / jnp.maximum(jnp.sum(p_c, axis=1, keepdims=True), 1e-30)
    p_cb = jnp.where(live, p_c, 0.0).astype(BF16)
    o_c = _dot(p_cb, vc)
    imp = jnp.sum(_dot(p_cb, cover_ref[...]), axis=0, keepdims=True)

    q_blk = PAST // SEL_BLOCK
    lane1 = lane8[0:1, :]
    forced = (lane1 == 0) | (lane1 > q_blk - N_LOCAL_SEL)
    score = jnp.where(lane1 <= q_blk, jnp.where(forced, -NEG, imp), NEG)
    sel = _rank_select(jnp.broadcast_to(score, (8, 128)), q_blk + 1, N_SEL)
    sel = jnp.where(lane8 <= q_blk, sel, 0.0)
    selm = _dot(sel.astype(BF16), es_ref[...]) > 0.5
    z = s_scr[8 * G_NSA:8 * G_NSA + 8, :]
    s_s = jnp.where(selm, z - slope_n * dist_f, NEG)
    zn = z_new[8 * G_NSA:8 * G_NSA + 8, :]
    m = jnp.maximum(jnp.max(s_s, axis=1, keepdims=True), zn)
    p = jnp.where(selm, jnp.exp(s_s - m), 0.0)
    pn = jnp.exp(zn - m)
    p_scr[8 * G_NSA:8 * G_NSA + 8, :] = p.astype(BF16)
    pnew.append(pn)
    inv.append(1.0 / jnp.maximum(jnp.sum(p, axis=1, keepdims=True) + pn, 1e-30))

    wl = lax.broadcasted_iota(jnp.int32, (8, WINDOW), 1)
    mask_w = wl >= 1
    wn = wn_ref[0]
    s_w = jnp.where(mask_w, _dot(qn, wkT_ref[0, 0].astype(BF16)) - slope_n * (float(WINDOW) - wl.astype(F32)), NEG)
    z_wn = jnp.sum(qn.astype(F32) * wn[:, 0:HEAD_DIM].astype(BF16).astype(F32), axis=1, keepdims=True)
    m = jnp.maximum(jnp.max(s_w, axis=1, keepdims=True), z_wn)
    p = jnp.where(mask_w, jnp.exp(s_w - m), 0.0)
    pn = jnp.exp(z_wn - m)
    o_w = (_dot_nt(p.astype(BF16), wvT_ref[0, 0].astype(BF16))
           + pn * wn[:, HEAD_DIM:2 * HEAD_DIM].astype(BF16).astype(F32))
    o_w = o_w / jnp.maximum(jnp.sum(p, axis=1, keepdims=True) + pn, 1e-30)

    n_blk = PAST // MOBA_BLOCK
    per_blk = MOBA_BLOCK // PAGE
    lane_km = lax.broadcasted_iota(jnp.int32, (BRANCH_WIDTH, 128), 1)
    kmT = jnp.zeros((BRANCH_WIDTH, 128), F32)
    for n in range(n_blk):
        kmT = jnp.where(lane_km == n, sum(ksum[n * per_blk:(n + 1) * per_blk]) * (1.0 / MOBA_BLOCK), kmT)
    gate = _dot(qm_ref[0, 8 * G_MOBA:8 * G_MOBA + 8, moba_rows], kmT.astype(BF16))
    gate = jnp.where(lane8 < n_blk, gate, NEG)
    selb = _rank_select(gate, n_blk, MOBA_TOPK)
    z = s_scr[8 * G_MOBA:8 * G_MOBA + 8, :]
    slope_m = _row_const(MOBA_SLOPES)
    sel_cols = jnp.concatenate([jnp.broadcast_to(selb[:, n:n + 1], (8, MOBA_BLOCK)) for n in range(n_blk)], axis=1)
    selk = sel_cols > 0.5
    s_m = jnp.where(selk, z - slope_m * dist_f, NEG)
    zn = z_new[8 * G_MOBA:8 * G_MOBA + 8, :]
    m = jnp.maximum(jnp.max(s_m, axis=1, keepdims=True), zn)
    p = jnp.where(selk, jnp.exp(s_m - m), 0.0)
    pn = jnp.exp(zn - m)
    p_scr[8 * G_MOBA:8 * G_MOBA + 8, :] = p.astype(BF16)
    pnew.append(pn)
    inv.append(1.0 / jnp.maximum(jnp.sum(p, axis=1, keepdims=True) + pn, 1e-30))

    lf = jnp.concatenate([pg[0, 0] for pg in lfp], axis=1)
    bias = _suffix_sum_excl(lf, u, 3) + hs[:, 3:4]
    z = s_scr[8 * G_FOX:8 * G_FOX + 8, :]
    s_f = z + bias
    zn = z_new[8 * G_FOX:8 * G_FOX + 8, :]
    m = jnp.maximum(jnp.max(s_f, axis=1, keepdims=True), zn)
    p = jnp.exp(s_f - m)
    pn = jnp.exp(zn - m)
    p_scr[8 * G_FOX:8 * G_FOX + 8, :] = p.astype(BF16)
    pnew.append(pn)
    inv.append(1.0 / jnp.maximum(jnp.sum(p, axis=1, keepdims=True) + pn, 1e-30))

    res = jnp.concatenate(pnew, axis=0) * vn_ref[0].astype(BF16).astype(F32)
    for j in range(N_PAGES):
        res = res + _dot_nt(p_scr[:, j * PAGE:(j + 1) * PAGE], vp[j][0, 0].astype(BF16))
    res = res * jnp.concatenate(inv, axis=0)

    def head_lanes(g, kv):
        return res[8 * g:8 * g + 8, kv * HEAD_DIM:(kv + 1) * HEAD_DIM]

    gsig = _sigmoid(hs)
    o_nsa = gsig[:, 0:1] * o_c + gsig[:, 1:2] * head_lanes(G_NSA, SEL_KV) + gsig[:, 2:3] * o_w
    pieces = [head_lanes(G_SB, h)[h:h + 1, :] for h in range(HEADS)]
    pieces += [o_nsa[h:h + 1, :] for h in range(HEADS)]
    pieces += [head_lanes(G_MOBA, MOBA_KV0 + h)[h:h + 1, :] for h in range(HEADS)]
    pieces += [head_lanes(G_FOX, FOX_KV0 + h)[h:h + 1, :] for h in range(HEADS)]
    o_ref[0] = jnp.concatenate(pieces, axis=1).astype(BF16)


def _query_rows(q):
    b = q.shape[0]
    kv_of = {G_SB: lambda h: h, G_NSA: lambda h: SEL_KV, G_MOBA: lambda h: MOBA_KV0 + h, G_FOX: lambda h: FOX_KV0 + h}
    rows = []
    for g in range(4):
        for h in range(HEADS):
            kv = kv_of[g](h)
            qh = q[:, (g * HEADS + h) * HEAD_DIM:(g * HEADS + h + 1) * HEAD_DIM]
            rows.append(jnp.pad(qh, ((0, 0), (kv * HEAD_DIM, (N_KV_HEADS - 1 - kv) * HEAD_DIM))))
        rows += [jnp.zeros((b, KV_COLS), q.dtype)] * (8 - HEADS)
    return jnp.stack(rows, axis=1)


def decode_attention(q, k_new, v_new, misc, ckT, cvT, clf, wkT, wvT, page_table, cmp, layer):
    b = q.shape[0]
    pos_k, w1k, w2k, pos_v, w1v, w2v = cmp
    qm = _query_rows(q)
    qn = jnp.pad(q[:, BRANCH_WIDTH:2 * BRANCH_WIDTH].reshape(b, HEADS, HEAD_DIM), ((0, 0), (0, 8 - HEADS), (0, 0)))
    gates = misc[:, 128:128 + 3 * HEADS].reshape(b, HEADS, 3)
    lf_new = misc[:, 128 + FGT_LANE0:128 + FGT_LANE0 + HEADS].reshape(b, HEADS, 1)
    hs = jnp.pad(jnp.concatenate([gates, lf_new], axis=2), ((0, 0), (0, 8 - HEADS), (0, 124)))
    wn = misc[:, 0:128].reshape(b, 1, 128)
    pos = jnp.stack([pos_k, pos_v])
    w1 = jnp.stack([w1k, w1v]).reshape(2, CMP_LEN, HEAD_DIM, CMP_HIDDEN).astype(BF16)
    w2 = jnp.stack([w2k, w2v]).astype(BF16)

    def page_map(j):
        return lambda i, pt: (layer, pt[i, j], 0, 0)

    per_b3 = lambda i, pt: (i, 0, 0)
    per_b4 = lambda i, pt: (layer, i, 0, 0)
    const2 = lambda i, pt: (0, 0)
    const3 = lambda i, pt: (0, 0, 0)
    const4 = lambda i, pt: (0, 0, 0, 0)
    in_specs = ([pl.BlockSpec((1, 1, KV_COLS, PAGE), page_map(j)) for j in range(N_PAGES)]
                + [pl.BlockSpec((1, 1, KV_COLS, PAGE), page_map(j)) for j in range(N_PAGES)]
                + [pl.BlockSpec((1, 1, 8, PAGE), page_map(j)) for j in range(N_PAGES)]
                + [pl.BlockSpec((1, Q_ROWS, KV_COLS), per_b3),
                   pl.BlockSpec((1, 8, HEAD_DIM), per_b3),
                   pl.BlockSpec((1, 1, KV_COLS), per_b3),
                   pl.BlockSpec((1, 1, KV_COLS), per_b3),
                   pl.BlockSpec((1, 8, 128), per_b3),
                   pl.BlockSpec((1, 1, 128), per_b3),
                   pl.BlockSpec((1, 1, HEAD_DIM, WINDOW), per_b4),
                   pl.BlockSpec((1, 1, HEAD_DIM, WINDOW), per_b4),
                   pl.BlockSpec((2, CMP_LEN, HEAD_DIM), const3),
                   pl.BlockSpec((2, CMP_LEN, HEAD_DIM, CMP_HIDDEN), const4),
                   pl.BlockSpec((2, CMP_HIDDEN, HEAD_DIM), const3),
                   pl.BlockSpec((SUM_CHUNK, SUM_CHUNK), const2),
                   pl.BlockSpec((128, 128), const2),
                   pl.BlockSpec((128, PAST), const2)])
    grid_spec = pltpu.PrefetchScalarGridSpec(
        num_scalar_prefetch=1,
        grid=(b,),
        in_specs=in_specs,
        out_specs=pl.BlockSpec((1, 1, D_MODEL), per_b3),
        scratch_shapes=[pltpu.VMEM((Q_ROWS, PAST), F32), pltpu.VMEM((Q_ROWS, PAST), BF16),
                        pltpu.VMEM((2, PAST, 128), F32)],
    )
    out = pl.pallas_call(
        _decode_body,
        grid_spec=grid_spec,
        out_shape=jax.ShapeDtypeStruct((b, 1, D_MODEL), BF16),
        compiler_params=_cparams(("arbitrary",)),
        name="decode_attention",
    )(page_table, *([ckT] * N_PAGES), *([cvT] * N_PAGES), *([clf] * N_PAGES),
      qm, qn, k_new.reshape(b, 1, KV_COLS), v_new.reshape(b, 1, KV_COLS), hs, wn, wkT, wvT,
      pos, w1, w2, _strict_upper_sum_matrix(SUM_CHUNK), _cover_matrix()[:128], _block_expand_matrix(SEL_BLOCK, PAST))
    return out.reshape(b, D_MODEL)


def sample_mixers(x, g_mix, w_parts, cmp, wb, wg, wo, ckT, cvT, clf, wkT, wvT, wk_buf, wv_buf, page_table, layer):
    b = x.shape[0]
    wq, wkT_w, wvT_w, wm, bias = w_parts
    h, q, k, v, misc = mixer_project_sample(x, g_mix, wq, wkT_w, wvT_w, wm, bias)
    o = decode_attention(q, k, v, misc, ckT, cvT, clf, wkT, wvT, page_table, cmp, layer)
    outs = [o[:, i * BRANCH_WIDTH:(i + 1) * BRANCH_WIDTH] for i in range(4)]
    x = merge_branches(x, h, outs, wb, wg, wo, tm=b)
    kw = misc[:, 0:HEAD_DIM].reshape(b, 1, 1, HEAD_DIM)
    vw = misc[:, HEAD_DIM:2 * HEAD_DIM].reshape(b, 1, 1, HEAD_DIM)
    state = (k.reshape(b, 1, N_KV_HEADS, HEAD_DIM), v.reshape(b, 1, N_KV_HEADS, HEAD_DIM),
             misc[:, 128 + FGT_LANE0:128 + FGT_LANE0 + HEADS].reshape(b, 1, HEADS),
             jnp.concatenate([wk_buf[:, 1:], kw], axis=1),
             jnp.concatenate([wv_buf[:, 1:], vw], axis=1))
    return x, state


PROMPT_TM = 512


def kernel(x_prompt, x_sample, cache_k, cache_v, cache_logf, state_win_k, state_win_v, page_table,
           g_ffn1, w_ffn1_gate, w_ffn1_up, w_ffn1_down, g_mix, w_in, b_fgt,
           cmp_pos_k, cmp_w1_k, cmp_w2_k, cmp_pos_v, cmp_w1_v, cmp_w2_v,
           w_branch, w_merge_gate, w_out, g_ffn2, w_ffn2_gate, w_ffn2_up, w_ffn2_down, g_final):
    B, S, d = x_prompt.shape
    depth = g_ffn1.shape[0]
    nb = x_sample.shape[0]
    n_phys = cache_k.shape[1]
    xp = x_prompt.reshape(B * S, d)
    xs = x_sample.reshape(nb, d)
    pst = [[] for _ in range(5)]
    sst = [[] for _ in range(5)]
    ckT = jnp.transpose(cache_k, (0, 1, 3, 4, 2)).reshape(depth, n_phys, KV_COLS, PAGE)
    cvT = jnp.transpose(cache_v, (0, 1, 3, 4, 2)).reshape(depth, n_phys, KV_COLS, PAGE)
    clf = jnp.pad(jnp.swapaxes(cache_logf, 2, 3), ((0, 0), (0, 0), (0, 8 - HEADS), (0, 0)))
    wkT = jnp.transpose(state_win_k, (0, 1, 3, 4, 2)).reshape(depth, nb, HEAD_DIM, WINDOW)
    wvT = jnp.transpose(state_win_v, (0, 1, 3, 4, 2)).reshape(depth, nb, HEAD_DIM, WINDOW)
    for l in range(depth):
        last = l == depth - 1
        f1 = (w_ffn1_gate[l].astype(BF16), w_ffn1_up[l].astype(BF16), w_ffn1_down[l].astype(BF16))
        f2 = (w_ffn2_gate[l].astype(BF16), w_ffn2_up[l].astype(BF16), w_ffn2_down[l].astype(BF16))
        w_parts = _split_w_in(w_in[l], b_fgt[l])
        cmp = (cmp_pos_k[l], cmp_w1_k[l], cmp_w2_k[l], cmp_pos_v[l], cmp_w1_v[l], cmp_w2_v[l])
        wb, wg, wo = w_branch[l].astype(BF16), w_merge_gate[l].astype(BF16), w_out[l].astype(BF16)
        xp = ffn_half(xp, g_ffn1[l], *f1, tm=PROMPT_TM)
        xp, st_p = prompt_mixers(xp, g_mix[l], w_parts, cmp, wb, wg, wo, batch=B, seq=S, tm=PROMPT_TM)
        xp = ffn_half(xp, g_ffn2[l], *f2, g_final if last else None, tm=PROMPT_TM)
        xs = ffn_half(xs, g_ffn1[l], *f1, tm=nb)
        xs, st_s = sample_mixers(xs, g_mix[l], w_parts, cmp, wb, wg, wo, ckT, cvT, clf, wkT, wvT,
                                 state_win_k[l], state_win_v[l], page_table, l)
        xs = ffn_half(xs, g_ffn2[l], *f2, g_final if last else None, tm=nb)
        for i in range(5):
            pst[i].append(st_p[i])
            sst[i].append(st_s[i])
    y_prompt = xp.reshape(B, S, d)
    y_sample = xs.reshape(nb, 1, d)
    return (y_prompt, y_sample) + tuple(jnp.stack(p) for p in pst) + tuple(jnp.stack(p) for p in sst)
```

```python
import functools

import numpy as np
import jax
import jax.numpy as jnp
from jax import lax
from jax.experimental import pallas as pl
from jax.experimental.pallas import tpu as pltpu

F32 = jnp.float32
BF16 = jnp.bfloat16

D_MODEL = 1024
HEAD_DIM = 64
HEADS = 4
BRANCH_WIDTH = HEADS * HEAD_DIM
N_KV_HEADS = 14
KV_COLS = N_KV_HEADS * HEAD_DIM
CMP_KV = 4
SEL_KV = 5
MOBA_KV0 = 6
FOX_KV0 = 10
CMP_STRIDE = 16
CMP_LEN = 32
CMP_HIDDEN = 128
SEL_BLOCK = 64
N_SEL = 16
N_LOCAL_SEL = 2
WINDOW = 512
MOBA_BLOCK = 256
MOBA_TOPK = 3
D_FF = 2816
RMS_EPS = 1e-6
Q_SCALE = HEAD_DIM ** -0.5
NEG = -1e30
MISC_COLS = 256
FGT_LANE0 = 12
NSA_SLOPES = (2.0 ** -1, 2.0 ** -3, 2.0 ** -5, 2.0 ** -7)
MOBA_SLOPES = (2.0 ** -2, 2.0 ** -4, 2.0 ** -6, 2.0 ** -8)
SB_FLOOR = -110.0

VMEM_LIMIT = 56 * 1024 * 1024


def _cparams(sem):
    return pltpu.CompilerParams(dimension_semantics=sem, vmem_limit_bytes=VMEM_LIMIT)


def _dot(a, b):
    return jnp.dot(a, b, preferred_element_type=F32)


def _dot_nt(a, b):
    return lax.dot_general(a, b, (((1,), (1,)), ((), ())), preferred_element_type=F32)


def _rms(x, g):
    return x * lax.rsqrt(jnp.mean(x * x, axis=-1, keepdims=True) + RMS_EPS) * g


def _softplus(z):
    return jnp.maximum(z, 0.0) + jnp.log(1.0 + jnp.exp(-jnp.abs(z)))


def _sigmoid(z):
    return 1.0 / (1.0 + jnp.exp(-z))


def _split2(a):
    hi = a.astype(BF16)
    lo = (a - hi.astype(F32)).astype(BF16)
    return hi, lo


def _split3(a):
    a1 = a.astype(BF16)
    r1 = a - a1.astype(F32)
    a2 = r1.astype(BF16)
    a3 = (r1 - a2.astype(F32)).astype(BF16)
    return a1, a2, a3


MXU_WIDTH = 256
FFN_CHUNK = 4 * MXU_WIDTH


def _ffn_body(x_ref, g_ref, wg_ref, wu_ref, wd_ref, gf_ref, o_ref, *, final_norm):
    x = x_ref[...]
    h = _rms(x, g_ref[...]).astype(BF16)
    f = wg_ref.shape[1]
    acc = None
    for c0 in range(0, f, FFN_CHUNK):
        c1 = min(c0 + FFN_CHUNK, f)
        a = _dot(h, wg_ref[:, c0:c1])
        u = _dot(h, wu_ref[:, c0:c1])
        act = ((a * _sigmoid(a)) * u).astype(BF16)
        part = _dot(act, wd_ref[c0:c1, :])
        acc = part if acc is None else acc + part
    y = x + 0.5 * acc
    if final_norm:
        y = _rms(y, gf_ref[...])
    o_ref[...] = y


def ffn_half(x, g, wg, wu, wd, g_final=None, *, tm):
    m, d = x.shape
    f = wg.shape[1]
    final_norm = g_final is not None
    gf = g_final if final_norm else g
    body = functools.partial(_ffn_body, final_norm=final_norm)
    return pl.pallas_call(
        body,
        grid=(m // tm,),
        in_specs=[
            pl.BlockSpec((tm, d), lambda i: (i, 0)),
            pl.BlockSpec((1, d), lambda i: (0, 0)),
            pl.BlockSpec((d, f), lambda i: (0, 0)),
            pl.BlockSpec((d, f), lambda i: (0, 0)),
            pl.BlockSpec((f, d), lambda i: (0, 0)),
            pl.BlockSpec((1, d), lambda i: (0, 0)),
        ],
        out_specs=pl.BlockSpec((tm, d), lambda i: (i, 0)),
        out_shape=jax.ShapeDtypeStruct((m, d), F32),
        compiler_params=_cparams(("parallel",)),
        name="ffn_half",
    )(x, g.reshape(1, d), wg, wu, wd, gf.reshape(1, d))


def _misc_and_logf(h, wm_ref, bias_ref, misc_ref):
    mz = _dot(h, wm_ref[...])
    grp = mz[:, 128:256]
    lane = lax.broadcasted_iota(jnp.int32, grp.shape, 1)
    is_f = (lane >= FGT_LANE0) & (lane < FGT_LANE0 + HEADS)
    xf = grp + bias_ref[...]
    lf = jnp.minimum(xf, 0.0) - jnp.log(1.0 + jnp.exp(-jnp.abs(xf)))
    lf = jnp.where(is_f, lf, 0.0)
    misc_ref[:, 0:128] = mz[:, 0:128]
    misc_ref[:, 128:256] = jnp.where(is_f, lf, grp)
    return lf


def _proj_prompt_body(x_ref, g_ref, wq_ref, wkT_ref, wvT_ref, wm_ref, bias_ref, tri_ref, k_all_ref, v_all_ref,
                      h_ref, q_ref, kT_ref, vT_ref, kTb_ref, vTb_ref, misc_ref, cum_ref, kmT_ref, carry_scr,
                      *, tm, tiles_per_seq):
    del k_all_ref, v_all_ref
    t = pl.program_id(0) % tiles_per_seq
    h = _rms(x_ref[...], g_ref[...]).astype(BF16)
    h_ref[...] = h
    q_ref[...] = (_dot(h, wq_ref[...]) * Q_SCALE).astype(BF16)
    kT = _dot_nt(wkT_ref[...], h)
    kT_ref[0, 0] = kT
    kTb_ref[0] = kT.astype(BF16)
    vT = _dot_nt(wvT_ref[...], h)
    vT_ref[0, 0] = vT
    vTb_ref[0] = vT.astype(BF16)
    lf = _misc_and_logf(h, wm_ref, bias_ref, misc_ref)

    @pl.when(t == 0)
    def _():
        carry_scr[...] = jnp.zeros_like(carry_scr)
        kmT_ref[...] = jnp.zeros_like(kmT_ref)

    a1, a2, a3 = _split3(lf)
    tri = tri_ref[...]
    cs = _dot(tri, a1) + _dot(tri, a2) + _dot(tri, a3) + carry_scr[0:1, :]
    cum_ref[...] = cs
    carry_scr[0:1, :] = cs[tm - 1:tm, :]

    lane = lax.broadcasted_iota(jnp.int32, (BRANCH_WIDTH, 128), 1)
    km = kmT_ref[0]
    for r in range(tm // MOBA_BLOCK):
        kblk = kT[MOBA_KV0 * HEAD_DIM:(MOBA_KV0 + HEADS) * HEAD_DIM, r * MOBA_BLOCK:(r + 1) * MOBA_BLOCK]
        col = jnp.sum(kblk, axis=1, keepdims=True) * (1.0 / MOBA_BLOCK)
        km = jnp.where(lane == t * (tm // MOBA_BLOCK) + r, col, km)
    kmT_ref[0] = km


def mixer_project_prompt(x, g, wq, wkT, wvT, wm, bias, k_all, v_all, *, layer, tm, batch, seq):
    m, d = x.shape
    tps = seq // tm
    tri = jnp.asarray(np.tril(np.ones((tm, tm), np.float32)), BF16)
    row = lambda i: (i, 0)
    const = lambda i: (0, 0)
    featmaj = lambda i: (i // tps, 0, i % tps)
    slab = lambda i: (layer, i // tps, 0, i % tps)
    kv_f32 = jax.ShapeDtypeStruct(k_all.shape, F32)
    kv_b16 = jax.ShapeDtypeStruct((batch, KV_COLS, seq), BF16)
    body = functools.partial(_proj_prompt_body, tm=tm, tiles_per_seq=tps)
    return pl.pallas_call(
        body,
        grid=(m // tm,),
        in_specs=[
            pl.BlockSpec((tm, d), row),
            pl.BlockSpec((1, d), const),
            pl.BlockSpec((d, d), const),
            pl.BlockSpec((KV_COLS, d), const),
            pl.BlockSpec((KV_COLS, d), const),
            pl.BlockSpec((d, MISC_COLS), const),
            pl.BlockSpec((1, 128), const),
            pl.BlockSpec((tm, tm), const),
            pl.BlockSpec(memory_space=pl.ANY),
            pl.BlockSpec(memory_space=pl.ANY),
        ],
        out_specs=[
            pl.BlockSpec((tm, d), row), pl.BlockSpec((tm, d), row),
            pl.BlockSpec((1, 1, KV_COLS, tm), slab), pl.BlockSpec((1, 1, KV_COLS, tm), slab),
            pl.BlockSpec((1, KV_COLS, tm), featmaj), pl.BlockSpec((1, KV_COLS, tm), featmaj),
            pl.BlockSpec((tm, MISC_COLS), row),
            pl.BlockSpec((tm, 128), row),
            pl.BlockSpec((1, BRANCH_WIDTH, 128), lambda i: (i // tps, 0, 0)),
        ],
        out_shape=[
            jax.ShapeDtypeStruct((m, d), BF16),
            jax.ShapeDtypeStruct((m, d), BF16),
            kv_f32, kv_f32, kv_b16, kv_b16,
            jax.ShapeDtypeStruct((m, MISC_COLS), F32),
            jax.ShapeDtypeStruct((m, 128), F32),
            jax.ShapeDtypeStruct((batch, BRANCH_WIDTH, 128), F32),
        ],
        scratch_shapes=[pltpu.VMEM((8, 128), F32)],
        input_output_aliases={8: 2, 9: 3},
        compiler_params=_cparams(("arbitrary",)),
        name="mixer_project_prompt",
    )(x, g.reshape(1, d), wq, wkT, wvT, wm, bias, tri, k_all, v_all)


def _proj_sample_body(x_ref, g_ref, wq_ref, wkT_ref, wvT_ref, wm_ref, bias_ref,
                      h_ref, q_ref, k_ref, v_ref, misc_ref):
    h = _rms(x_ref[...], g_ref[...]).astype(BF16)
    h_ref[...] = h
    q_ref[...] = (_dot(h, wq_ref[...]) * Q_SCALE).astype(BF16)
    k_ref[...] = _dot_nt(h, wkT_ref[...])
    v_ref[...] = _dot_nt(h, wvT_ref[...])
    _misc_and_logf(h, wm_ref, bias_ref, misc_ref)


def mixer_project_sample(x, g, wq, wkT, wvT, wm, bias):
    m, d = x.shape
    row = lambda i: (0, 0)
    full = lambda shape: pl.BlockSpec(shape, row)
    return pl.pallas_call(
        _proj_sample_body,
        grid=(1,),
        in_specs=[full((m, d)), full((1, d)), full((d, d)), full((KV_COLS, d)), full((KV_COLS, d)),
                  full((d, MISC_COLS)), full((1, 128))],
        out_specs=[full((m, d)), full((m, d)), full((m, KV_COLS)), full((m, KV_COLS)), full((m, MISC_COLS))],
        out_shape=[
            jax.ShapeDtypeStruct((m, d), BF16),
            jax.ShapeDtypeStruct((m, d), BF16),
            jax.ShapeDtypeStruct((m, KV_COLS), F32),
            jax.ShapeDtypeStruct((m, KV_COLS), F32),
            jax.ShapeDtypeStruct((m, MISC_COLS), F32),
        ],
        compiler_params=_cparams(("arbitrary",)),
        name="mixer_project_sample",
    )(x, g.reshape(1, d), wq, wkT, wvT, wm, bias)


def _gelu_tanh(x):
    return 0.5 * x * (1.0 + jnp.tanh(np.sqrt(2.0 / np.pi).astype(np.float32) * (x + 0.044715 * (x * x * x))))


def _compress_body(xs_ref, pos_ref, w1a_ref, w1b_ref, w2_ref, o_ref):
    xs = xs_ref[0, 0]
    n16 = xs.shape[0]
    pos = pos_ref[0]
    a = _dot((xs + pos[0:1, :]).astype(BF16), w1a_ref[0])
    b = _dot((xs + pos[1:2, :]).astype(BF16), w1b_ref[0])
    hid = a + pltpu.roll(b, n16 - 1, 0)
    o_ref[0, 0] = _dot(_gelu_tanh(hid).astype(BF16), w2_ref[0])


def nsa_compress(xs, pos, w1, w2):
    _, b, n16, cw = xs.shape
    pos2 = pos.reshape(2, 2, cw)
    w1a = w1[:, :cw].astype(BF16)
    w1b = w1[:, cw:].astype(BF16)
    return pl.pallas_call(
        _compress_body,
        grid=(2, b),
        in_specs=[
            pl.BlockSpec((1, 1, n16, cw), lambda t, i: (t, i, 0, 0)),
            pl.BlockSpec((1, 2, cw), lambda t, i: (t, 0, 0)),
            pl.BlockSpec((1, cw, CMP_HIDDEN), lambda t, i: (t, 0, 0)),
            pl.BlockSpec((1, cw, CMP_HIDDEN), lambda t, i: (t, 0, 0)),
            pl.BlockSpec((1, CMP_HIDDEN, HEAD_DIM), lambda t, i: (t, 0, 0)),
        ],
        out_specs=pl.BlockSpec((1, 1, n16, HEAD_DIM), lambda t, i: (t, i, 0, 0)),
        out_shape=jax.ShapeDtypeStruct((2, b, n16, HEAD_DIM), F32),
        compiler_params=_cparams(("parallel", "parallel")),
        name="nsa_compress",
    )(xs, pos2, w1a, w1b, w2.astype(BF16))


TQ = 256
TK = 256


def _head_spec(s, head):
    return pl.BlockSpec((1, HEAD_DIM, s), lambda i, j, head=head: (i, head, 0))


def _pair_spec(s, pair):
    return pl.BlockSpec((1, 2 * HEAD_DIM, s), lambda i, j, pair=pair: (i, pair, 0))


def _merge_pairs(accs):
    lane = lax.broadcasted_iota(jnp.int32, accs[0].shape, 1)
    lo = lane < HEAD_DIM
    return jnp.concatenate([jnp.where(lo, accs[0], accs[1]), jnp.where(lo, accs[2], accs[3])], axis=1)


def _wide(x):
    return jnp.concatenate([x] * (TK // 128), axis=1)


def _online(s, mask, m, l, acc, vT):
    m_new = jnp.maximum(m, jnp.max(s, axis=1, keepdims=True))
    alpha = jnp.exp(m - m_new)
    p = jnp.exp(s - _wide(m_new))
    if mask is not None:
        p = jnp.where(mask, p, 0.0)
    l = alpha * l + jnp.sum(p, axis=1, keepdims=True)
    acc = alpha * acc + _dot_nt(p.astype(BF16), vT)
    return m_new, l, acc


def _softmax_init():
    return tuple((jnp.full((TQ, 128), NEG, F32), jnp.zeros((TQ, 128), F32), jnp.zeros((TQ, 128), F32))
                 for _ in range(HEADS))


def _finish(l, acc):
    return acc / jnp.maximum(l, 1e-30)


def _tile_iotas(qi):
    row = qi * TQ + lax.broadcasted_iota(jnp.int32, (TQ, TK), 0)
    lcol = lax.broadcasted_iota(jnp.int32, (TQ, TK), 1)
    return row, lcol


def _own_key_tile(qi):
    return (qi * TQ) // TK


def _sb_body(q_ref, k0_ref, k1_ref, k2_ref, k3_ref, v0_ref, v1_ref, u_ref, o_ref):
    qi = pl.program_id(1)
    k_refs = (k0_ref, k1_ref, k2_ref, k3_ref)
    v_refs = (v0_ref, v1_ref)
    row, lcol = _tile_iotas(qi)
    u = u_ref[...]
    qs = [q_ref[0, :, h * HEAD_DIM:(h + 1) * HEAD_DIM] for h in range(HEADS)]

    def tile(k0, before, accs, runs):
        new_a, new_r = [], []
        for h in range(HEADS):
            z = _dot(qs[h], k_refs[h][0, :, pl.ds(k0, TK)])
            sp = _softplus(z)
            lk = -sp if before is None else jnp.where(before, -sp, 0.0)
            hi, lo = _split2(lk)
            sums = _dot(hi, u) + _dot(lo, u)
            loc = sums[:, 0:TK]
            w = jnp.exp(z - sp + loc + _wide(runs[h]))
            if before is not None:
                w = jnp.where(before, w, 0.0)
            new_a.append(accs[h] + _dot_nt(w.astype(BF16), v_refs[h // 2][0, :, pl.ds(k0, TK)]))
            new_r.append(runs[h] + sums[:, TK:TK + 128])
        return tuple(new_a), tuple(new_r)

    zero_a = tuple(jnp.zeros((TQ, 128), F32) for _ in range(HEADS))
    zero_r = tuple(jnp.zeros((TQ, 128), F32) for _ in range(HEADS))
    kd = _own_key_tile(qi)
    q0 = pl.multiple_of(kd * TK, TK)
    accs, runs = tile(q0, (q0 + lcol) < row, zero_a, zero_r)

    def cond(c):
        jj, _, runs = c
        top = jnp.max(jnp.maximum(jnp.maximum(runs[0], runs[1]), jnp.maximum(runs[2], runs[3])))
        return jnp.logical_and(jj <= kd, top > SB_FLOOR)

    def body(c):
        jj, accs, runs = c
        accs, runs = tile(pl.multiple_of((kd - jj) * TK, TK), None, accs, runs)
        return jj + 1, accs, runs

    _, accs, _ = lax.while_loop(cond, body, (jnp.int32(1), accs, runs))
    o_ref[0] = _merge_pairs(accs).astype(BF16)


def _strict_upper_sum_matrix(n, total_cols=0):
    u = np.concatenate([np.tril(np.ones((n, n), np.float32), -1), np.ones((n, total_cols), np.float32)], axis=1)
    return jnp.asarray(u, BF16)


def sb_attention(q, kTb, vTb):
    b, s, _ = q.shape
    return pl.pallas_call(
        _sb_body,
        grid=(b, s // TQ),
        in_specs=[pl.BlockSpec((1, TQ, BRANCH_WIDTH), lambda i, j: (i, j, 0))]
        + [_head_spec(s, h) for h in range(HEADS)]
        + [_pair_spec(s, 0), _pair_spec(s, 1)]
        + [pl.BlockSpec((TK, TK + 128), lambda i, j: (0, 0))],
        out_specs=pl.BlockSpec((1, TQ, BRANCH_WIDTH), lambda i, j: (i, j, 0)),
        out_shape=jax.ShapeDtypeStruct((b, s, BRANCH_WIDTH), BF16),
        compiler_params=_cparams(("parallel", "arbitrary")),
        name="sb_attention",
    )(q, kTb, kTb, kTb, kTb, vTb, vTb, _strict_upper_sum_matrix(TK, 128))


def _fox_body(q_ref, k0_ref, k1_ref, k2_ref, k3_ref, v0_ref, v1_ref, cq_ref, ck_ref, o_ref):
    qi = pl.program_id(1)
    k_refs = (k0_ref, k1_ref, k2_ref, k3_ref)
    v_refs = (v0_ref, v1_ref)
    row, lcol = _tile_iotas(qi)
    qs = [q_ref[0, :, h * HEAD_DIM:(h + 1) * HEAD_DIM] for h in range(HEADS)]
    cqs = [jnp.broadcast_to(cq_ref[0, :, FGT_LANE0 + h:FGT_LANE0 + h + 1], (TQ, 128)) for h in range(HEADS)]

    def tile(k0, mask, carry):
        out = []
        for h in range(HEADS):
            m, l, acc = carry[h]
            z = _dot(qs[h], k_refs[h][0, :, pl.ds(k0, TK)])
            s = z + _wide(cqs[h]) - ck_ref[0, h:h + 1, pl.ds(k0, TK)]
            if mask is not None:
                s = jnp.where(mask, s, NEG)
            out.append(_online(s, mask, m, l, acc, v_refs[h // 2][0, :, pl.ds(k0, TK)]))
        return tuple(out)

    kd = _own_key_tile(qi)
    carry = lax.fori_loop(0, kd, lambda j, c: tile(pl.multiple_of(j * TK, TK), None, c), _softmax_init())
    q0 = pl.multiple_of(kd * TK, TK)
    carry = tile(q0, (q0 + lcol) <= row, carry)
    o_ref[0] = _merge_pairs([_finish(l, acc) for (_, l, acc) in carry]).astype(BF16)


def fox_attention(q, kTb, vTb, cum, cumT):
    b, s, _ = q.shape
    return pl.pallas_call(
        _fox_body,
        grid=(b, s // TQ),
        in_specs=[pl.BlockSpec((1, TQ, BRANCH_WIDTH), lambda i, j: (i, j, 3))]
        + [_head_spec(s, FOX_KV0 + h) for h in range(HEADS)]
        + [_pair_spec(s, FOX_KV0 // 2), _pair_spec(s, FOX_KV0 // 2 + 1)]
        + [pl.BlockSpec((1, TQ, 128), lambda i, j: (i, j, 0)),
           pl.BlockSpec((1, 8, s), lambda i, j: (i, 0, 0))],
        out_specs=pl.BlockSpec((1, TQ, BRANCH_WIDTH), lambda i, j: (i, j, 0)),
        out_shape=jax.ShapeDtypeStruct((b, s, BRANCH_WIDTH), BF16),
        compiler_params=_cparams(("parallel", "arbitrary")),
        name="fox_attention",
    )(q, kTb, kTb, kTb, kTb, vTb, vTb, cum, cumT)


def _rank_select(score, n_cols, k):
    lane = lax.broadcasted_iota(jnp.int32, score.shape, 1)
    rank = jnp.zeros(score.shape, F32)
    for c in range(n_cols):
        col = score[:, c:c + 1]
        ahead = jnp.where(col > score, 1.0, jnp.where(col == score, jnp.where(lane > c, 1.0, 0.0), 0.0))
        rank = rank + ahead
    return jnp.where(rank < k, 1.0, 0.0)


def _rank_select_rows(score, n_cols, k):
    n_pad = -(-n_cols // 8) * 8
    st = score.T[0:n_pad, :]
    cand = lax.broadcasted_iota(jnp.int32, st.shape, 0)
    rank = jnp.zeros(st.shape, F32)
    for c in range(n_cols):
        ref = st[c:c + 1, :]
        ahead = jnp.where(ref > st, 1.0, jnp.where(ref == st, jnp.where(cand > c, 1.0, 0.0), 0.0))
        rank = rank + ahead
    sel = jnp.where(rank < k, 1.0, 0.0)
    if n_pad < 128:
        sel = jnp.concatenate([sel, jnp.zeros((128 - n_pad, st.shape[1]), F32)], axis=0)
    return sel.T


def _moba_body(q_ref, k0_ref, k1_ref, k2_ref, k3_ref, v0_ref, v1_ref, kmT_ref, em_ref, o_ref):
    qi = pl.program_id(1)
    k_refs = (k0_ref, k1_ref, k2_ref, k3_ref)
    v_refs = (v0_ref, v1_ref)
    row, lcol = _tile_iotas(qi)
    lane = lax.broadcasted_iota(jnp.int32, (TQ, 128), 1)
    n_blocks = k0_ref.shape[2] // MOBA_BLOCK
    kd = _own_key_tile(qi)
    qs = [q_ref[0, :, h * HEAD_DIM:(h + 1) * HEAD_DIM] for h in range(HEADS)]
    sels = []
    for h in range(HEADS):
        gate = _dot(qs[h], kmT_ref[0, h * HEAD_DIM:(h + 1) * HEAD_DIM, :].astype(BF16))
        gate = jnp.where(lane < kd, gate, NEG)
        sel = _rank_select_rows(gate, n_blocks, MOBA_TOPK)
        sels.append(jnp.where(lane < kd, sel, 0.0).astype(BF16))

    def past_tile(j, carry):
        k0 = pl.multiple_of(j * TK, TK)
        distf = (row - (k0 + lcol)).astype(F32)
        out = []
        for h in range(HEADS):
            m, l, acc = carry[h]
            z = _dot(qs[h], k_refs[h][0, :, pl.ds(k0, TK)])
            bit = _dot(sels[h], em_ref[:, pl.ds(pl.multiple_of(j * 128, 128), 128)])
            s = z - MOBA_SLOPES[h] * distf
            m_new = jnp.maximum(m, jnp.where(bit > 0.5, jnp.max(s, axis=1, keepdims=True), NEG))
            alpha = jnp.exp(m - m_new)
            p = jnp.exp(jnp.minimum(s - _wide(m_new), 0.0))
            l = alpha * l + bit * jnp.sum(p, axis=1, keepdims=True)
            acc = alpha * acc + bit * _dot_nt(p.astype(BF16), v_refs[h // 2][0, :, pl.ds(k0, TK)])
            out.append((m_new, l, acc))
        return tuple(out)

    def own_tile(carry):
        k0 = pl.multiple_of(kd * TK, TK)
        dist = row - (k0 + lcol)
        mask = dist >= 0
        distf = dist.astype(F32)
        out = []
        for h in range(HEADS):
            m, l, acc = carry[h]
            z = _dot(qs[h], k_refs[h][0, :, pl.ds(k0, TK)])
            s = jnp.where(mask, z - MOBA_SLOPES[h] * distf, NEG)
            out.append(_online(s, mask, m, l, acc, v_refs[h // 2][0, :, pl.ds(k0, TK)]))
        return tuple(out)

    carry = own_tile(lax.fori_loop(0, kd, past_tile, _softmax_init()))
    o_ref[0] = _merge_pairs([_finish(l, acc) for (_, l, acc) in carry]).astype(BF16)


def _block_expand_matrix(block, s):
    e = (np.arange(128)[:, None] == (np.arange(s)[None, :] // block)).astype(np.float32)
    return jnp.asarray(e, BF16)


def moba_attention(q, kTb, vTb, kmT):
    b, s, _ = q.shape
    n_blocks = s // MOBA_BLOCK
    return pl.pallas_call(
        _moba_body,
        grid=(b, s // TQ),
        in_specs=[pl.BlockSpec((1, TQ, BRANCH_WIDTH), lambda i, j: (i, j, 2))]
        + [_head_spec(s, MOBA_KV0 + h) for h in range(HEADS)]
        + [_pair_spec(s, MOBA_KV0 // 2), _pair_spec(s, MOBA_KV0 // 2 + 1)]
        + [pl.BlockSpec((1, BRANCH_WIDTH, 128), lambda i, j: (i, 0, 0)),
           pl.BlockSpec((128, n_blocks * 128), lambda i, j: (0, 0))],
        out_specs=pl.BlockSpec((1, TQ, BRANCH_WIDTH), lambda i, j: (i, j, 0)),
        out_shape=jax.ShapeDtypeStruct((b, s, BRANCH_WIDTH), BF16),
        compiler_params=_cparams(("parallel", "arbitrary")),
        name="moba_attention",
    )(q, kTb, kTb, kTb, kTb, vTb, vTb, kmT, _block_expand_matrix(128, n_blocks * 128))


def _nsa_body(q_ref, kT_ref, vT_ref, kcT_ref, vc_ref, kwT_ref, vwT_ref, g_ref, cover_ref, es_ref, o_ref, st_scr,
              *, n_cmp):
    qi = pl.program_id(1)
    s_len = kT_ref.shape[2]
    n_sel_blocks = s_len // SEL_BLOCK
    row, lcol = _tile_iotas(qi)
    qs = [q_ref[0, :, h * HEAD_DIM:(h + 1) * HEAD_DIM] for h in range(HEADS)]

    ncp = kcT_ref.shape[2]
    crow = qi * TQ + lax.broadcasted_iota(jnp.int32, (TQ, ncp), 0)
    cn = lax.broadcasted_iota(jnp.int32, (TQ, ncp), 1)
    dist_c = crow - (cn * CMP_STRIDE + (CMP_LEN - 1))
    mask_c = jnp.where(cn < n_cmp, dist_c, -1) >= 0
    dist_cf = dist_c.astype(F32)
    kcT = kcT_ref[0]
    vc = vc_ref[0]
    cover = cover_ref[...]
    o_c = []
    imp = jnp.zeros((TQ, 128), F32)
    for h in range(HEADS):
        s = jnp.where(mask_c, _dot(qs[h], kcT) - NSA_SLOPES[h] * dist_cf, NEG)
        m = jnp.max(s, axis=1, keepdims=True)
        p = jnp.where(mask_c, jnp.exp(s - m), 0.0)
        p = p / jnp.maximum(jnp.sum(p, axis=1, keepdims=True), 1e-30)
        pb = p.astype(BF16)
        o_c.append(_dot(pb, vc))
        imp = imp + _dot(pb, cover)

    lane = lax.broadcasted_iota(jnp.int32, (TQ, 128), 1)
    q_blk = (qi * TQ + lax.broadcasted_iota(jnp.int32, (TQ, 128), 0)) // SEL_BLOCK
    forced = (lane == 0) | (lane > q_blk - N_LOCAL_SEL)
    score = jnp.where(lane <= q_blk, jnp.where(forced, -NEG, imp), NEG)
    sel_f = jnp.where(lane <= q_blk, _rank_select_rows(score, n_sel_blocks, N_SEL), 0.0)
    sel = sel_f.astype(BF16)
    picked = jnp.max(sel_f, axis=0, keepdims=True)
    lane1 = lane[0:1, :]

    for h in range(HEADS):
        st_scr[0, h] = jnp.full((TQ, 128), NEG, F32)
        st_scr[1, h] = jnp.zeros((TQ, 128), F32)
        st_scr[2, h] = jnp.zeros((TQ, 128), F32)

    def sel_tile(k0, diag):
        kT = kT_ref[0, :, pl.ds(k0, TK)]
        vT = vT_ref[0, :, pl.ds(k0, TK)]
        dist = row - (k0 + lcol)
        selm = _dot(sel, es_ref[:, pl.ds(k0, TK)])
        if diag:
            mask = jnp.where(dist >= 0, selm, 0.0) > 0.5
        else:
            mask = selm > 0.5
        distf = dist.astype(F32)
        for h in range(HEADS):
            s = jnp.where(mask, _dot(qs[h], kT) - NSA_SLOPES[h] * distf, NEG)
            m, l, acc = _online(s, mask, st_scr[0, h], st_scr[1, h], st_scr[2, h], vT)
            st_scr[0, h] = m
            st_scr[1, h] = l
            st_scr[2, h] = acc

    def past_tile(j, _):
        blk0 = j * (TK // SEL_BLOCK)
        in_tile = (lane1 >= blk0) & (lane1 < blk0 + TK // SEL_BLOCK)
        hit = jnp.max(jnp.where(in_tile, picked, 0.0))

        @pl.when(hit > 0.5)
        def _():
            sel_tile(pl.multiple_of(j * TK, TK), False)

        return 0

    kd = _own_key_tile(qi)
    lax.fori_loop(0, kd, past_tile, 0)
    sel_tile(pl.multiple_of(kd * TK, TK), True)
    o_s = [_finish(st_scr[1, h], st_scr[2, h]) for h in range(HEADS)]

    def win_body(j, carry):
        k0 = pl.multiple_of(j * TK, TK)
        kT = kwT_ref[0, :, pl.ds(k0, TK)]
        vT = vwT_ref[0, :, pl.ds(k0, TK)]
        dist = row - (k0 + lcol)
        mask = jnp.where(dist >= 0, dist, WINDOW) < WINDOW
        distf = dist.astype(F32)
        out = []
        for h in range(HEADS):
            m, l, acc = carry[h]
            s = jnp.where(mask, _dot(qs[h], kT) - NSA_SLOPES[h] * distf, NEG)
            out.append(_online(s, mask, m, l, acc, vT))
        return tuple(out)

    res_w = lax.fori_loop(jnp.maximum(kd - WINDOW // TK, 0), kd + 1, win_body, _softmax_init())
    o_w = [_finish(l, acc)[:, 0:HEAD_DIM] for (_, l, acc) in res_w]

    g = _sigmoid(g_ref[0])
    outs = []
    for h in range(HEADS):
        g0 = g[:, 3 * h + 0:3 * h + 1]
        g1 = g[:, 3 * h + 1:3 * h + 2]
        g2 = g[:, 3 * h + 2:3 * h + 3]
        outs.append(g0 * o_c[h] + g1 * o_s[h][:, HEAD_DIM:2 * HEAD_DIM] + g2 * o_w[h])
    o_ref[0] = jnp.concatenate(outs, axis=1).astype(BF16)


def _cover_matrix():
    n = np.arange(256)[:, None] * CMP_STRIDE
    m = np.arange(128)[None, :] * SEL_BLOCK
    return jnp.asarray(((n < m + SEL_BLOCK) & (n + CMP_LEN > m)).astype(np.float32), BF16)


def nsa_attention(q, kTb, vTb, kcT, vc, kwT, vwT, misc, n_cmp):
    b, s, _ = q.shape
    ncp = kcT.shape[2]
    body = functools.partial(_nsa_body, n_cmp=n_cmp)
    return pl.pallas_call(
        body,
        grid=(b, s // TQ),
        in_specs=[
            pl.BlockSpec((1, TQ, BRANCH_WIDTH), lambda i, j: (i, j, 1)),
            _head_spec(s, SEL_KV),
            _pair_spec(s, SEL_KV // 2),
            pl.BlockSpec((1, HEAD_DIM, ncp), lambda i, j: (i, 0, 0)),
            pl.BlockSpec((1, ncp, HEAD_DIM), lambda i, j: (i, 0, 0)),
            pl.BlockSpec((1, HEAD_DIM, s), lambda i, j: (i, 0, 0)),
            pl.BlockSpec((1, 2 * HEAD_DIM, s), lambda i, j: (i, 0, 0)),
            pl.BlockSpec((1, TQ, 128), lambda i, j: (i, j, 1)),
            pl.BlockSpec((ncp, 128), lambda i, j: (0, 0)),
            pl.BlockSpec((128, s), lambda i, j: (0, 0)),
        ],
        out_specs=pl.BlockSpec((1, TQ, BRANCH_WIDTH), lambda i, j: (i, j, 0)),
        out_shape=jax.ShapeDtypeStruct((b, s, BRANCH_WIDTH), BF16),
        scratch_shapes=[pltpu.VMEM((3, HEADS, TQ, 128), F32)],
        compiler_params=_cparams(("parallel", "arbitrary")),
        name="nsa_attention",
    )(q, kTb, vTb, kcT, vc, kwT, vwT, misc, _cover_matrix()[:ncp], _block_expand_matrix(SEL_BLOCK, s))


def _merge_body(x_ref, h_ref, o0_ref, o1_ref, o2_ref, o3_ref, wb_ref, wg_ref, wo_ref, out_ref):
    h = h_ref[...]
    d = x_ref.shape[1]
    mix = None
    for n, o_ref in enumerate((o0_ref, o1_ref, o2_ref, o3_ref)):
        gate = _sigmoid(_dot(h, wg_ref[:, n * d:(n + 1) * d]))
        term = gate * _dot(o_ref[...], wb_ref[n])
        mix = term if mix is None else mix + term
    out_ref[...] = x_ref[...] + _dot(mix.astype(BF16), wo_ref[...])


def merge_branches(x, h, outs, wb, wg, wo, *, tm):
    m, d = x.shape
    row = lambda i: (i, 0)
    return pl.pallas_call(
        _merge_body,
        grid=(m // tm,),
        in_specs=[
            pl.BlockSpec((tm, d), row),
            pl.BlockSpec((tm, d), row),
            pl.BlockSpec((tm, BRANCH_WIDTH), row),
            pl.BlockSpec((tm, BRANCH_WIDTH), row),
            pl.BlockSpec((tm, BRANCH_WIDTH), row),
            pl.BlockSpec((tm, BRANCH_WIDTH), row),
            pl.BlockSpec((4, BRANCH_WIDTH, d), lambda i: (0, 0, 0)),
            pl.BlockSpec((d, 4 * d), lambda i: (0, 0)),
            pl.BlockSpec((d, d), lambda i: (0, 0)),
        ],
        out_specs=pl.BlockSpec((tm, d), row),
        out_shape=jax.ShapeDtypeStruct((m, d), F32),
        compiler_params=_cparams(("parallel",)),
        name="merge_branches",
    )(x, h, *outs, wb, wg, wo)


def _split_w_in(w_in, b_fgt):
    wq = w_in[:, :D_MODEL].astype(BF16)
    wkT = w_in[:, D_MODEL:D_MODEL + KV_COLS].T.astype(BF16)
    wvT = w_in[:, D_MODEL + KV_COLS:D_MODEL + 2 * KV_COLS].T.astype(BF16)
    rest = w_in[:, D_MODEL + 2 * KV_COLS:]
    wm = jnp.pad(rest, ((0, 0), (0, MISC_COLS - rest.shape[1]))).astype(BF16)
    bias = jnp.zeros((1, 128), F32).at[0, FGT_LANE0:FGT_LANE0 + HEADS].set(b_fgt)
    return wq, wkT, wvT, wm, bias


def prompt_branches(x, g_mix, w_parts, cmp, k_all, v_all, *, layer, batch, seq, tm):
    wq, wkT, wvT, wm, bias = w_parts
    h, q, k_all, v_all, kTb, vTb, misc, cum, kmT = mixer_project_prompt(
        x, g_mix, wq, wkT, wvT, wm, bias, k_all, v_all, layer=layer, tm=tm, batch=batch, seq=seq)
    B, S = batch, seq
    q3 = q.reshape(B, S, D_MODEL)
    misc3 = misc.reshape(B, S, MISC_COLS)
    cum3 = cum.reshape(B, S, 128)

    o_sb = sb_attention(q3, kTb, vTb)

    pos_k, w1k, w2k, pos_v, w1v, w2v = cmp
    n16 = S // CMP_STRIDE
    cw = CMP_STRIDE * HEAD_DIM
    cmp_rows = slice(CMP_KV * HEAD_DIM, (CMP_KV + 1) * HEAD_DIM)
    xs = jnp.stack([jnp.swapaxes(k_all[layer, :, cmp_rows, :], 1, 2).reshape(B, n16, cw),
                    jnp.swapaxes(v_all[layer, :, cmp_rows, :], 1, 2).reshape(B, n16, cw)])
    kvc = nsa_compress(xs, jnp.stack([pos_k, pos_v]), jnp.stack([w1k, w1v]), jnp.stack([w2k, w2v]))
    kcT = jnp.swapaxes(kvc[0], 1, 2).astype(BF16)
    vc = kvc[1].astype(BF16)
    kwT = jnp.swapaxes(misc3[:, :, 0:HEAD_DIM], 1, 2).astype(BF16)
    vwT = jnp.swapaxes(misc3[:, :, HEAD_DIM:2 * HEAD_DIM], 1, 2).astype(BF16)
    vwT = jnp.pad(vwT, ((0, 0), (0, HEAD_DIM), (0, 0)))
    o_nsa = nsa_attention(q3, kTb, vTb, kcT, vc, kwT, vwT, misc3, n16 - 1)

    o_moba = moba_attention(q3, kTb, vTb, kmT)

    cumT = jnp.swapaxes(cum3[:, :, FGT_LANE0:FGT_LANE0 + HEADS], 1, 2)
    cumT = jnp.pad(cumT, ((0, 0), (0, 8 - HEADS), (0, 0)))
    o_fox = fox_attention(q3, kTb, vTb, cum3, cumT)

    outs = [o.reshape(B * S, BRANCH_WIDTH) for o in (o_sb, o_nsa, o_moba, o_fox)]
    wb_len = min(WINDOW, S)
    state = (misc3[:, :, 128 + FGT_LANE0:128 + FGT_LANE0 + HEADS],
             misc3[:, S - wb_len:, 0:HEAD_DIM].reshape(B, wb_len, 1, HEAD_DIM),
             misc3[:, S - wb_len:, HEAD_DIM:2 * HEAD_DIM].reshape(B, wb_len, 1, HEAD_DIM))
    return h, outs, state, k_all, v_all


def kv_heads(a_all):
    depth, b, _, s = a_all.shape
    return jnp.transpose(a_all.reshape(depth, b, N_KV_HEADS, HEAD_DIM, s), (0, 1, 4, 2, 3))


def prompt_mixers(x, g_mix, w_parts, cmp, wb, wg, wo, k_all, v_all, *, layer, batch, seq, tm):
    h, outs, state, k_all, v_all = prompt_branches(x, g_mix, w_parts, cmp, k_all, v_all,
                                                   layer=layer, batch=batch, seq=seq, tm=tm)
    return merge_branches(x, h, outs, wb, wg, wo, tm=tm), state, k_all, v_all


PAGE = 128
N_PAGES = 16
PAST = PAGE * N_PAGES
Q_ROWS = 32
SUM_CHUNK = 256
G_SB, G_NSA, G_MOBA, G_FOX = 0, 1, 2, 3


def _row_const(vals):
    r = lax.broadcasted_iota(jnp.int32, (8, 1), 0)
    out = jnp.zeros((8, 1), F32)
    for i, v in enumerate(vals):
        out = jnp.where(r == i, v, out)
    return out


def _suffix_sum_excl(x, u, n_split):
    n = x.shape[1] // SUM_CHUNK
    carry = jnp.zeros((x.shape[0], 1), F32)
    outs = [None] * n
    for c in reversed(range(n)):
        xc = x[:, c * SUM_CHUNK:(c + 1) * SUM_CHUNK]
        parts = _split3(xc) if n_split == 3 else _split2(xc)
        loc = _dot(parts[0], u)
        for p in parts[1:]:
            loc = loc + _dot(p, u)
        outs[c] = loc + carry
        carry = carry + (loc[:, 0:1] + xc[:, 0:1])
    return jnp.concatenate(outs, axis=1)


def _decode_body(pt_ref, *refs):
    kp = refs[0:N_PAGES]
    vp = refs[N_PAGES:2 * N_PAGES]
    lfp = refs[2 * N_PAGES:3 * N_PAGES]
    (qm_ref, qn_ref, kn_ref, vn_ref, hs_ref, wn_ref, wkT_ref, wvT_ref, pos_ref, w1_ref, w2_ref,
     u_ref, cover_ref, es_ref, o_ref, s_scr, p_scr, cmp_scr) = refs[3 * N_PAGES:]
    del pt_ref
    qm = qm_ref[0]
    lane8 = lax.broadcasted_iota(jnp.int32, (8, 128), 1)
    row8 = lax.broadcasted_iota(jnp.int32, (8, 128), 0)
    live = row8[:, 0:1] < HEADS
    moba_rows = slice(MOBA_KV0 * HEAD_DIM, (MOBA_KV0 + HEADS) * HEAD_DIM)
    cmp_rows = slice(CMP_KV * HEAD_DIM, (CMP_KV + 2) * HEAD_DIM)

    ksum = []
    for j in range(N_PAGES):
        kT = kp[j][0, 0]
        s_scr[:, j * PAGE:(j + 1) * PAGE] = _dot(qm, kT.astype(BF16))
        ksum.append(jnp.sum(kT[moba_rows, :], axis=1, keepdims=True))
        cmp_scr[0, j * PAGE:(j + 1) * PAGE, :] = kT[cmp_rows, :].T
        cmp_scr[1, j * PAGE:(j + 1) * PAGE, :] = vp[j][0, 0, cmp_rows, :].T
    kn = kn_ref[0].astype(BF16).astype(F32)
    z_new = jnp.sum(qm.astype(F32) * kn, axis=1, keepdims=True)

    pos_f = lax.broadcasted_iota(jnp.int32, (8, PAST), 1).astype(F32)
    dist_f = float(PAST) - pos_f
    u = u_ref[...]
    hs = hs_ref[0]

    z = s_scr[8 * G_SB:8 * G_SB + 8, :]
    sp = _softplus(z)
    later = _suffix_sum_excl(-sp, u, 2)
    p_scr[8 * G_SB:8 * G_SB + 8, :] = jnp.exp(z - sp + later).astype(BF16)
    pnew = [jnp.zeros((8, 1), F32)]
    inv = [jnp.ones((8, 1), F32)]

    kvc = []
    for t in range(2):
        a = jnp.zeros((PAGE, CMP_HIDDEN), F32)
        b = jnp.zeros((PAGE, CMP_HIDDEN), F32)
        for r in range(CMP_STRIDE):
            xr = cmp_scr[t, pl.ds(r, PAST // CMP_STRIDE, stride=CMP_STRIDE), :][:, 0:HEAD_DIM]
            a = a + _dot((xr + pos_ref[t, r:r + 1, :]).astype(BF16), w1_ref[t, r])
            b = b + _dot((xr + pos_ref[t, CMP_STRIDE + r:CMP_STRIDE + r + 1, :]).astype(BF16),
                         w1_ref[t, CMP_STRIDE + r])
        hid = a + pltpu.roll(b, PAGE - 1, 0)
        kvc.append(_dot(_gelu_tanh(hid).astype(BF16), w2_ref[t]).astype(BF16))
    kc, vc = kvc
    n_cmp = PAST // CMP_STRIDE - 1

    qn = qn_ref[0]
    slope_n = _row_const(NSA_SLOPES)
    dist_c = float(PAST - (CMP_LEN - 1)) - float(CMP_STRIDE) * lane8.astype(F32)
    mask_c = lane8 < n_cmp
    s_c = jnp.where(mask_c, _dot_nt(qn, kc) - slope_n * dist_c, NEG)
    p_c = jnp.where(mask_c, jnp.exp(s_c - jnp.max(s_c, axis=1, keepdims=True)), 0.0)
    p_c = p_c / jnp.maximum(jnp.sum(p_c, axis=1, keepdims=True), 1e-30)
    p_cb = jnp.where(live, p_c, 0.0).astype(BF16)
    o_c = _dot(p_cb, vc)
    imp = jnp.sum(_dot(p_cb, cover_ref[...]), axis=0, keepdims=True)

    q_blk = PAST // SEL_BLOCK
    lane1 = lane8[0:1, :]
    forced = (lane1 == 0) | (lane1 > q_blk - N_LOCAL_SEL)
    score = jnp.where(lane1 <= q_blk, jnp.where(forced, -NEG, imp), NEG)
    sel = _rank_select(jnp.broadcast_to(score, (8, 128)), q_blk + 1, N_SEL)
    sel = jnp.where(lane8 <= q_blk, sel, 0.0)
    selm = _dot(sel.astype(BF16), es_ref[...]) > 0.5
    z = s_scr[8 * G_NSA:8 * G_NSA + 8, :]
    s_s = jnp.where(selm, z - slope_n * dist_f, NEG)
    zn = z_new[8 * G_NSA:8 * G_NSA + 8, :]
    m = jnp.maximum(jnp.max(s_s, axis=1, keepdims=True), zn)
    p = jnp.where(selm, jnp.exp(s_s - m), 0.0)
    pn = jnp.exp(zn - m)
    p_scr[8 * G_NSA:8 * G_NSA + 8, :] = p.astype(BF16)
    pnew.append(pn)
    inv.append(1.0 / jnp.maximum(jnp.sum(p, axis=1, keepdims=True) + pn, 1e-30))

    wl = lax.broadcasted_iota(jnp.int32, (8, WINDOW), 1)
    mask_w = wl >= 1
    wn = wn_ref[0]
    s_w = jnp.where(mask_w, _dot(qn, wkT_ref[0, 0].astype(BF16)) - slope_n * (float(WINDOW) - wl.astype(F32)), NEG)
    z_wn = jnp.sum(qn.astype(F32) * wn[:, 0:HEAD_DIM].astype(BF16).astype(F32), axis=1, keepdims=True)
    m = jnp.maximum(jnp.max(s_w, axis=1, keepdims=True), z_wn)
    p = jnp.where(mask_w, jnp.exp(s_w - m), 0.0)
    pn = jnp.exp(z_wn - m)
    o_w = (_dot_nt(p.astype(BF16), wvT_ref[0, 0].astype(BF16))
           + pn * wn[:, HEAD_DIM:2 * HEAD_DIM].astype(BF16).astype(F32))
    o_w = o_w / jnp.maximum(jnp.sum(p, axis=1, keepdims=True) + pn, 1e-30)

    n_blk = PAST // MOBA_BLOCK
    per_blk = MOBA_BLOCK // PAGE
    lane_km = lax.broadcasted_iota(jnp.int32, (BRANCH_WIDTH, 128), 1)
    kmT = jnp.zeros((BRANCH_WIDTH, 128), F32)
    for n in range(n_blk):
        kmT = jnp.where(lane_km == n, sum(ksum[n * per_blk:(n + 1) * per_blk]) * (1.0 / MOBA_BLOCK), kmT)
    gate = _dot(qm_ref[0, 8 * G_MOBA:8 * G_MOBA + 8, moba_rows], kmT.astype(BF16))
    gate = jnp.where(lane8 < n_blk, gate, NEG)
    selb = _rank_select(gate, n_blk, MOBA_TOPK)
    z = s_scr[8 * G_MOBA:8 * G_MOBA + 8, :]
    slope_m = _row_const(MOBA_SLOPES)
    sel_cols = jnp.concatenate([jnp.broadcast_to(selb[:, n:n + 1], (8, MOBA_BLOCK)) for n in range(n_blk)], axis=1)
    selk = sel_cols > 0.5
    s_m = jnp.where(selk, z - slope_m * dist_f, NEG)
    zn = z_new[8 * G_MOBA:8 * G_MOBA + 8, :]
    m = jnp.maximum(jnp.max(s_m, axis=1, keepdims=True), zn)
    p = jnp.where(selk, jnp.exp(s_m - m), 0.0)
    pn = jnp.exp(zn - m)
    p_scr[8 * G_MOBA:8 * G_MOBA + 8, :] = p.astype(BF16)
    pnew.append(pn)
    inv.append(1.0 / jnp.maximum(jnp.sum(p, axis=1, keepdims=True) + pn, 1e-30))

    lf = jnp.concatenate([pg[0, 0] for pg in lfp], axis=1)
    bias = _suffix_sum_excl(lf, u, 3) + hs[:, 3:4]
    z = s_scr[8 * G_FOX:8 * G_FOX + 8, :]
    s_f = z + bias
    zn = z_new[8 * G_FOX:8 * G_FOX + 8, :]
    m = jnp.maximum(jnp.max(s_f, axis=1, keepdims=True), zn)
    p = jnp.exp(s_f - m)
    pn = jnp.exp(zn - m)
    p_scr[8 * G_FOX:8 * G_FOX + 8, :] = p.astype(BF16)
    pnew.append(pn)
    inv.append(1.0 / jnp.maximum(jnp.sum(p, axis=1, keepdims=True) + pn, 1e-30))

    res = jnp.concatenate(pnew, axis=0) * vn_ref[0].astype(BF16).astype(F32)
    for j in range(N_PAGES):
        res = res + _dot_nt(p_scr[:, j * PAGE:(j + 1) * PAGE], vp[j][0, 0].astype(BF16))
    res = res * jnp.concatenate(inv, axis=0)

    def head_lanes(g, kv):
        return res[8 * g:8 * g + 8, kv * HEAD_DIM:(kv + 1) * HEAD_DIM]

    gsig = _sigmoid(hs)
    o_nsa = gsig[:, 0:1] * o_c + gsig[:, 1:2] * head_lanes(G_NSA, SEL_KV) + gsig[:, 2:3] * o_w
    pieces = [head_lanes(G_SB, h)[h:h + 1, :] for h in range(HEADS)]
    pieces += [o_nsa[h:h + 1, :] for h in range(HEADS)]
    pieces += [head_lanes(G_MOBA, MOBA_KV0 + h)[h:h + 1, :] for h in range(HEADS)]
    pieces += [head_lanes(G_FOX, FOX_KV0 + h)[h:h + 1, :] for h in range(HEADS)]
    o_ref[0] = jnp.concatenate(pieces, axis=1).astype(BF16)


def _query_rows(q):
    b = q.shape[0]
    kv_of = {G_SB: lambda h: h, G_NSA: lambda h: SEL_KV, G_MOBA: lambda h: MOBA_KV0 + h, G_FOX: lambda h: FOX_KV0 + h}
    rows = []
    for g in range(4):
        for h in range(HEADS):
            kv = kv_of[g](h)
            qh = q[:, (g * HEADS + h) * HEAD_DIM:(g * HEADS + h + 1) * HEAD_DIM]
            rows.append(jnp.pad(qh, ((0, 0), (kv * HEAD_DIM, (N_KV_HEADS - 1 - kv) * HEAD_DIM))))
        rows += [jnp.zeros((b, KV_COLS), q.dtype)] * (8 - HEADS)
    return jnp.stack(rows, axis=1)


def decode_attention(q, k_new, v_new, misc, ckT, cvT, clf, wkT, wvT, page_table, cmp, layer):
    b = q.shape[0]
    pos_k, w1k, w2k, pos_v, w1v, w2v = cmp
    qm = _query_rows(q)
    qn = jnp.pad(q[:, BRANCH_WIDTH:2 * BRANCH_WIDTH].reshape(b, HEADS, HEAD_DIM), ((0, 0), (0, 8 - HEADS), (0, 0)))
    gates = misc[:, 128:128 + 3 * HEADS].reshape(b, HEADS, 3)
    lf_new = misc[:, 128 + FGT_LANE0:128 + FGT_LANE0 + HEADS].reshape(b, HEADS, 1)
    hs = jnp.pad(jnp.concatenate([gates, lf_new], axis=2), ((0, 0), (0, 8 - HEADS), (0, 124)))
    wn = misc[:, 0:128].reshape(b, 1, 128)
    pos = jnp.stack([pos_k, pos_v])
    w1 = jnp.stack([w1k, w1v]).reshape(2, CMP_LEN, HEAD_DIM, CMP_HIDDEN).astype(BF16)
    w2 = jnp.stack([w2k, w2v]).astype(BF16)

    def page_map(j):
        return lambda i, pt: (layer, pt[i, j], 0, 0)

    per_b3 = lambda i, pt: (i, 0, 0)
    per_b4 = lambda i, pt: (layer, i, 0, 0)
    const2 = lambda i, pt: (0, 0)
    const3 = lambda i, pt: (0, 0, 0)
    const4 = lambda i, pt: (0, 0, 0, 0)
    in_specs = ([pl.BlockSpec((1, 1, KV_COLS, PAGE), page_map(j)) for j in range(N_PAGES)]
                + [pl.BlockSpec((1, 1, KV_COLS, PAGE), page_map(j)) for j in range(N_PAGES)]
                + [pl.BlockSpec((1, 1, 8, PAGE), page_map(j)) for j in range(N_PAGES)]
                + [pl.BlockSpec((1, Q_ROWS, KV_COLS), per_b3),
                   pl.BlockSpec((1, 8, HEAD_DIM), per_b3),
                   pl.BlockSpec((1, 1, KV_COLS), per_b3),
                   pl.BlockSpec((1, 1, KV_COLS), per_b3),
                   pl.BlockSpec((1, 8, 128), per_b3),
                   pl.BlockSpec((1, 1, 128), per_b3),
                   pl.BlockSpec((1, 1, HEAD_DIM, WINDOW), per_b4),
                   pl.BlockSpec((1, 1, HEAD_DIM, WINDOW), per_b4),
                   pl.BlockSpec((2, CMP_LEN, HEAD_DIM), const3),
                   pl.BlockSpec((2, CMP_LEN, HEAD_DIM, CMP_HIDDEN), const4),
                   pl.BlockSpec((2, CMP_HIDDEN, HEAD_DIM), const3),
                   pl.BlockSpec((SUM_CHUNK, SUM_CHUNK), const2),
                   pl.BlockSpec((128, 128), const2),
                   pl.BlockSpec((128, PAST), const2)])
    grid_spec = pltpu.PrefetchScalarGridSpec(
        num_scalar_prefetch=1,
        grid=(b,),
        in_specs=in_specs,
        out_specs=pl.BlockSpec((1, 1, D_MODEL), per_b3),
        scratch_shapes=[pltpu.VMEM((Q_ROWS, PAST), F32), pltpu.VMEM((Q_ROWS, PAST), BF16),
                        pltpu.VMEM((2, PAST, 128), F32)],
    )
    out = pl.pallas_call(
        _decode_body,
        grid_spec=grid_spec,
        out_shape=jax.ShapeDtypeStruct((b, 1, D_MODEL), BF16),
        compiler_params=_cparams(("arbitrary",)),
        name="decode_attention",
    )(page_table, *([ckT] * N_PAGES), *([cvT] * N_PAGES), *([clf] * N_PAGES),
      qm, qn, k_new.reshape(b, 1, KV_COLS), v_new.reshape(b, 1, KV_COLS), hs, wn, wkT, wvT,
      pos, w1, w2, _strict_upper_sum_matrix(SUM_CHUNK), _cover_matrix()[:128], _block_expand_matrix(SEL_BLOCK, PAST))
    return out.reshape(b, D_MODEL)


def sample_mixers(x, g_mix, w_parts, cmp, wb, wg, wo, ckT, cvT, clf, wkT, wvT, wk_buf, wv_buf, page_table, layer):
    b = x.shape[0]
    wq, wkT_w, wvT_w, wm, bias = w_parts
    h, q, k, v, misc = mixer_project_sample(x, g_mix, wq, wkT_w, wvT_w, wm, bias)
    o = decode_attention(q, k, v, misc, ckT, cvT, clf, wkT, wvT, page_table, cmp, layer)
    outs = [o[:, i * BRANCH_WIDTH:(i + 1) * BRANCH_WIDTH] for i in range(4)]
    x = merge_branches(x, h, outs, wb, wg, wo, tm=b)
    kw = misc[:, 0:HEAD_DIM].reshape(b, 1, 1, HEAD_DIM)
    vw = misc[:, HEAD_DIM:2 * HEAD_DIM].reshape(b, 1, 1, HEAD_DIM)
    state = (k.reshape(b, 1, N_KV_HEADS, HEAD_DIM), v.reshape(b, 1, N_KV_HEADS, HEAD_DIM),
             misc[:, 128 + FGT_LANE0:128 + FGT_LANE0 + HEADS].reshape(b, 1, HEADS),
             jnp.concatenate([wk_buf[:, 1:], kw], axis=1),
             jnp.concatenate([wv_buf[:, 1:], vw], axis=1))
    return x, state


PROMPT_TM = 512


def kernel(x_prompt, x_sample, cache_k, cache_v, cache_logf, state_win_k, state_win_v, page_table,
           g_ffn1, w_ffn1_gate, w_ffn1_up, w_ffn1_down, g_mix, w_in, b_fgt,
           cmp_pos_k, cmp_w1_k, cmp_w2_k, cmp_pos_v, cmp_w1_v, cmp_w2_v,
           w_branch, w_merge_gate, w_out, g_ffn2, w_ffn2_gate, w_ffn2_up, w_ffn2_down, g_final):
    B, S, d = x_prompt.shape
    depth = g_ffn1.shape[0]
    nb = x_sample.shape[0]
    n_phys = cache_k.shape[1]
    xp = x_prompt.reshape(B * S, d)
    xs = x_sample.reshape(nb, d)
    pst = [[] for _ in range(3)]
    sst = [[] for _ in range(5)]
    pk_all = jnp.zeros((depth, B, KV_COLS, S), F32)
    pv_all = jnp.zeros((depth, B, KV_COLS, S), F32)
    ckT = jnp.transpose(cache_k, (0, 1, 3, 4, 2)).reshape(depth, n_phys, KV_COLS, PAGE)
    cvT = jnp.transpose(cache_v, (0, 1, 3, 4, 2)).reshape(depth, n_phys, KV_COLS, PAGE)
    clf = jnp.pad(jnp.swapaxes(cache_logf, 2, 3), ((0, 0), (0, 0), (0, 8 - HEADS), (0, 0)))
    wkT = jnp.transpose(state_win_k, (0, 1, 3, 4, 2)).reshape(depth, nb, HEAD_DIM, WINDOW)
    wvT = jnp.transpose(state_win_v, (0, 1, 3, 4, 2)).reshape(depth, nb, HEAD_DIM, WINDOW)
    for l in range(depth):
        last = l == depth - 1
        f1 = (w_ffn1_gate[l].astype(BF16), w_ffn1_up[l].astype(BF16), w_ffn1_down[l].astype(BF16))
        f2 = (w_ffn2_gate[l].astype(BF16), w_ffn2_up[l].astype(BF16), w_ffn2_down[l].astype(BF16))
        w_parts = _split_w_in(w_in[l], b_fgt[l])
        cmp = (cmp_pos_k[l], cmp_w1_k[l], cmp_w2_k[l], cmp_pos_v[l], cmp_w1_v[l], cmp_w2_v[l])
        wb, wg, wo = w_branch[l].astype(BF16), w_merge_gate[l].astype(BF16), w_out[l].astype(BF16)
        xp = ffn_half(xp, g_ffn1[l], *f1, tm=PROMPT_TM)
        xp, st_p, pk_all, pv_all = prompt_mixers(xp, g_mix[l], w_parts, cmp, wb, wg, wo, pk_all, pv_all,
                                                 layer=l, batch=B, seq=S, tm=PROMPT_TM)
        xp = ffn_half(xp, g_ffn2[l], *f2, g_final if last else None, tm=PROMPT_TM)
        xs = ffn_half(xs, g_ffn1[l], *f1, tm=nb)
        xs, st_s = sample_mixers(xs, g_mix[l], w_parts, cmp, wb, wg, wo, ckT, cvT, clf, wkT, wvT,
                                 state_win_k[l], state_win_v[l], page_table, l)
        xs = ffn_half(xs, g_ffn2[l], *f2, g_final if last else None, tm=nb)
        for i in range(3):
            pst[i].append(st_p[i])
        for i in range(5):
            sst[i].append(st_s[i])
    y_prompt = xp.reshape(B, S, d)
    y_sample = xs.reshape(nb, 1, d)
    return ((y_prompt, y_sample, kv_heads(pk_all), kv_heads(pv_all)) + tuple(jnp.stack(p) for p in pst)
            + tuple(jnp.stack(p) for p in sst))
```

```python
import functools

import numpy as np
import jax
import jax.numpy as jnp
from jax import lax
from jax.experimental import pallas as pl
from jax.experimental.pallas import tpu as pltpu

F32 = jnp.float32
BF16 = jnp.bfloat16

D_MODEL = 1024
HEAD_DIM = 64
HEADS = 4
BRANCH_WIDTH = HEADS * HEAD_DIM
N_KV_HEADS = 14
KV_COLS = N_KV_HEADS * HEAD_DIM
CMP_KV = 4
SEL_KV = 5
MOBA_KV0 = 6
FOX_KV0 = 10
CMP_STRIDE = 16
CMP_LEN = 32
CMP_HIDDEN = 128
SEL_BLOCK = 64
N_SEL = 16
N_LOCAL_SEL = 2
WINDOW = 512
MOBA_BLOCK = 256
MOBA_TOPK = 3
D_FF = 2816
RMS_EPS = 1e-6
Q_SCALE = HEAD_DIM ** -0.5
NEG = -1e30
MISC_COLS = 256
FGT_LANE0 = 12
NSA_SLOPES = (2.0 ** -1, 2.0 ** -3, 2.0 ** -5, 2.0 ** -7)
MOBA_SLOPES = (2.0 ** -2, 2.0 ** -4, 2.0 ** -6, 2.0 ** -8)
SB_FLOOR = -110.0

VMEM_LIMIT = 56 * 1024 * 1024


def _cparams(sem):
    return pltpu.CompilerParams(dimension_semantics=sem, vmem_limit_bytes=VMEM_LIMIT)


def _dot(a, b):
    return jnp.dot(a, b, preferred_element_type=F32)


def _dot_nt(a, b):
    return lax.dot_general(a, b, (((1,), (1,)), ((), ())), preferred_element_type=F32)


def _rms(x, g):
    return x * lax.rsqrt(jnp.mean(x * x, axis=-1, keepdims=True) + RMS_EPS) * g


def _softplus(z):
    return jnp.maximum(z, 0.0) + jnp.log(1.0 + jnp.exp(-jnp.abs(z)))


def _sigmoid(z):
    return 1.0 / (1.0 + jnp.exp(-z))


def _split2(a):
    hi = a.astype(BF16)
    lo = (a - hi.astype(F32)).astype(BF16)
    return hi, lo


def _split3(a):
    a1 = a.astype(BF16)
    r1 = a - a1.astype(F32)
    a2 = r1.astype(BF16)
    a3 = (r1 - a2.astype(F32)).astype(BF16)
    return a1, a2, a3


MXU_WIDTH = 256
FFN_CHUNK = 4 * MXU_WIDTH


def _ffn_body(x_ref, g_ref, wg_ref, wu_ref, wd_ref, gf_ref, o_ref, *, final_norm):
    x = x_ref[...]
    h = _rms(x, g_ref[...]).astype(BF16)
    f = wg_ref.shape[1]
    acc = None
    for c0 in range(0, f, FFN_CHUNK):
        c1 = min(c0 + FFN_CHUNK, f)
        a = _dot(h, wg_ref[:, c0:c1])
        u = _dot(h, wu_ref[:, c0:c1])
        act = ((a * _sigmoid(a)) * u).astype(BF16)
        part = _dot(act, wd_ref[c0:c1, :])
        acc = part if acc is None else acc + part
    y = x + 0.5 * acc
    if final_norm:
        y = _rms(y, gf_ref[...])
    o_ref[...] = y


def ffn_half(x, g, wg, wu, wd, g_final=None, *, tm):
    m, d = x.shape
    f = wg.shape[1]
    final_norm = g_final is not None
    gf = g_final if final_norm else g
    body = functools.partial(_ffn_body, final_norm=final_norm)
    return pl.pallas_call(
        body,
        grid=(m // tm,),
        in_specs=[
            pl.BlockSpec((tm, d), lambda i: (i, 0)),
            pl.BlockSpec((1, d), lambda i: (0, 0)),
            pl.BlockSpec((d, f), lambda i: (0, 0)),
            pl.BlockSpec((d, f), lambda i: (0, 0)),
            pl.BlockSpec((f, d), lambda i: (0, 0)),
            pl.BlockSpec((1, d), lambda i: (0, 0)),
        ],
        out_specs=pl.BlockSpec((tm, d), lambda i: (i, 0)),
        out_shape=jax.ShapeDtypeStruct((m, d), F32),
        compiler_params=_cparams(("parallel",)),
        name="ffn_half",
    )(x, g.reshape(1, d), wg, wu, wd, gf.reshape(1, d))


def _misc_and_logf(h, wm_ref, bias_ref, misc_ref):
    mz = _dot(h, wm_ref[...])
    grp = mz[:, 128:256]
    lane = lax.broadcasted_iota(jnp.int32, grp.shape, 1)
    is_f = (lane >= FGT_LANE0) & (lane < FGT_LANE0 + HEADS)
    xf = grp + bias_ref[...]
    lf = jnp.minimum(xf, 0.0) - jnp.log(1.0 + jnp.exp(-jnp.abs(xf)))
    lf = jnp.where(is_f, lf, 0.0)
    misc_ref[:, 0:128] = mz[:, 0:128]
    misc_ref[:, 128:256] = jnp.where(is_f, lf, grp)
    return lf


CUM_CHUNK = 128


def _proj_prompt_body(x_ref, g_ref, wq_ref, wkT_ref, wvT_ref, wm_ref, bias_ref, tri_ref, k_all_ref, v_all_ref,
                      h_ref, q_ref, kT_ref, vT_ref, kTb_ref, vTb_ref, misc_ref, cum_ref, kmT_ref, carry_scr,
                      *, tm, tiles_per_seq):
    del k_all_ref, v_all_ref
    t = pl.program_id(0) % tiles_per_seq
    h = _rms(x_ref[...], g_ref[...]).astype(BF16)
    h_ref[...] = h
    q_ref[...] = (_dot(h, wq_ref[...]) * Q_SCALE).astype(BF16)
    kT = _dot_nt(wkT_ref[...], h)
    kT_ref[0, 0] = kT
    kTb_ref[0] = kT.astype(BF16)
    vT = _dot_nt(wvT_ref[...], h)
    vT_ref[0, 0] = vT
    vTb_ref[0] = vT.astype(BF16)
    lf = _misc_and_logf(h, wm_ref, bias_ref, misc_ref)

    @pl.when(t == 0)
    def _():
        carry_scr[...] = jnp.zeros_like(carry_scr)
        kmT_ref[...] = jnp.zeros_like(kmT_ref)

    tri = tri_ref[...]
    carry = carry_scr[0:1, :]
    for c in range(tm // CUM_CHUNK):
        a1, a2, a3 = _split3(lf[c * CUM_CHUNK:(c + 1) * CUM_CHUNK, :])
        cs = _dot(tri, a1) + _dot(tri, a2) + _dot(tri, a3) + carry
        cum_ref[c * CUM_CHUNK:(c + 1) * CUM_CHUNK, :] = cs
        carry = cs[CUM_CHUNK - 1:CUM_CHUNK, :]
    carry_scr[0:1, :] = carry

    lane = lax.broadcasted_iota(jnp.int32, (BRANCH_WIDTH, 128), 1)
    km = kmT_ref[0]
    for r in range(tm // MOBA_BLOCK):
        kblk = kT[MOBA_KV0 * HEAD_DIM:(MOBA_KV0 + HEADS) * HEAD_DIM, r * MOBA_BLOCK:(r + 1) * MOBA_BLOCK]
        col = jnp.sum(kblk, axis=1, keepdims=True) * (1.0 / MOBA_BLOCK)
        km = jnp.where(lane == t * (tm // MOBA_BLOCK) + r, col, km)
    kmT_ref[0] = km


def mixer_project_prompt(x, g, wq, wkT, wvT, wm, bias, k_all, v_all, *, layer, tm, batch, seq):
    m, d = x.shape
    tps = seq // tm
    tri = jnp.asarray(np.tril(np.ones((CUM_CHUNK, CUM_CHUNK), np.float32)), BF16)
    row = lambda i: (i, 0)
    const = lambda i: (0, 0)
    featmaj = lambda i: (i // tps, 0, i % tps)
    slab = lambda i: (layer, i // tps, 0, i % tps)
    kv_f32 = jax.ShapeDtypeStruct(k_all.shape, F32)
    kv_b16 = jax.ShapeDtypeStruct((batch, KV_COLS, seq), BF16)
    body = functools.partial(_proj_prompt_body, tm=tm, tiles_per_seq=tps)
    return pl.pallas_call(
        body,
        grid=(m // tm,),
        in_specs=[
            pl.BlockSpec((tm, d), row),
            pl.BlockSpec((1, d), const),
            pl.BlockSpec((d, d), const),
            pl.BlockSpec((KV_COLS, d), const),
            pl.BlockSpec((KV_COLS, d), const),
            pl.BlockSpec((d, MISC_COLS), const),
            pl.BlockSpec((1, 128), const),
            pl.BlockSpec((CUM_CHUNK, CUM_CHUNK), const),
            pl.BlockSpec(memory_space=pl.ANY),
            pl.BlockSpec(memory_space=pl.ANY),
        ],
        out_specs=[
            pl.BlockSpec((tm, d), row), pl.BlockSpec((tm, d), row),
            pl.BlockSpec((1, 1, KV_COLS, tm), slab), pl.BlockSpec((1, 1, KV_COLS, tm), slab),
            pl.BlockSpec((1, KV_COLS, tm), featmaj), pl.BlockSpec((1, KV_COLS, tm), featmaj),
            pl.BlockSpec((tm, MISC_COLS), row),
            pl.BlockSpec((tm, 128), row),
            pl.BlockSpec((1, BRANCH_WIDTH, 128), lambda i: (i // tps, 0, 0)),
        ],
        out_shape=[
            jax.ShapeDtypeStruct((m, d), BF16),
            jax.ShapeDtypeStruct((m, d), BF16),
            kv_f32, kv_f32, kv_b16, kv_b16,
            jax.ShapeDtypeStruct((m, MISC_COLS), F32),
            jax.ShapeDtypeStruct((m, 128), F32),
            jax.ShapeDtypeStruct((batch, BRANCH_WIDTH, 128), F32),
        ],
        scratch_shapes=[pltpu.VMEM((8, 128), F32)],
        input_output_aliases={8: 2, 9: 3},
        compiler_params=_cparams(("arbitrary",)),
        name="mixer_project_prompt",
    )(x, g.reshape(1, d), wq, wkT, wvT, wm, bias, tri, k_all, v_all)


def _proj_sample_body(x_ref, g_ref, wq_ref, wkT_ref, wvT_ref, wm_ref, bias_ref,
                      h_ref, q_ref, k_ref, v_ref, misc_ref):
    h = _rms(x_ref[...], g_ref[...]).astype(BF16)
    h_ref[...] = h
    q_ref[...] = (_dot(h, wq_ref[...]) * Q_SCALE).astype(BF16)
    k_ref[...] = _dot_nt(h, wkT_ref[...])
    v_ref[...] = _dot_nt(h, wvT_ref[...])
    _misc_and_logf(h, wm_ref, bias_ref, misc_ref)


def mixer_project_sample(x, g, wq, wkT, wvT, wm, bias):
    m, d = x.shape
    row = lambda i: (0, 0)
    full = lambda shape: pl.BlockSpec(shape, row)
    return pl.pallas_call(
        _proj_sample_body,
        grid=(1,),
        in_specs=[full((m, d)), full((1, d)), full((d, d)), full((KV_COLS, d)), full((KV_COLS, d)),
                  full((d, MISC_COLS)), full((1, 128))],
        out_specs=[full((m, d)), full((m, d)), full((m, KV_COLS)), full((m, KV_COLS)), full((m, MISC_COLS))],
        out_shape=[
            jax.ShapeDtypeStruct((m, d), BF16),
            jax.ShapeDtypeStruct((m, d), BF16),
            jax.ShapeDtypeStruct((m, KV_COLS), F32),
            jax.ShapeDtypeStruct((m, KV_COLS), F32),
            jax.ShapeDtypeStruct((m, MISC_COLS), F32),
        ],
        compiler_params=_cparams(("arbitrary",)),
        name="mixer_project_sample",
    )(x, g.reshape(1, d), wq, wkT, wvT, wm, bias)


def _gelu_tanh(x):
    return 0.5 * x * (1.0 + jnp.tanh(np.sqrt(2.0 / np.pi).astype(np.float32) * (x + 0.044715 * (x * x * x))))


def _compress_body(xs_ref, pos_ref, w1a_ref, w1b_ref, w2_ref, o_ref):
    xs = xs_ref[0, 0]
    n16 = xs.shape[0]
    pos = pos_ref[0]
    a = _dot((xs + pos[0:1, :]).astype(BF16), w1a_ref[0])
    b = _dot((xs + pos[1:2, :]).astype(BF16), w1b_ref[0])
    hid = a + pltpu.roll(b, n16 - 1, 0)
    o_ref[0, 0] = _dot(_gelu_tanh(hid).astype(BF16), w2_ref[0])


def nsa_compress(xs, pos, w1, w2):
    _, b, n16, cw = xs.shape
    pos2 = pos.reshape(2, 2, cw)
    w1a = w1[:, :cw].astype(BF16)
    w1b = w1[:, cw:].astype(BF16)
    return pl.pallas_call(
        _compress_body,
        grid=(2, b),
        in_specs=[
            pl.BlockSpec((1, 1, n16, cw), lambda t, i: (t, i, 0, 0)),
            pl.BlockSpec((1, 2, cw), lambda t, i: (t, 0, 0)),
            pl.BlockSpec((1, cw, CMP_HIDDEN), lambda t, i: (t, 0, 0)),
            pl.BlockSpec((1, cw, CMP_HIDDEN), lambda t, i: (t, 0, 0)),
            pl.BlockSpec((1, CMP_HIDDEN, HEAD_DIM), lambda t, i: (t, 0, 0)),
        ],
        out_specs=pl.BlockSpec((1, 1, n16, HEAD_DIM), lambda t, i: (t, i, 0, 0)),
        out_shape=jax.ShapeDtypeStruct((2, b, n16, HEAD_DIM), F32),
        compiler_params=_cparams(("parallel", "parallel")),
        name="nsa_compress",
    )(xs, pos2, w1a, w1b, w2.astype(BF16))


TQ = 256
TK = 256
FOX_TK = 2 * TK


def _head_spec(s, head):
    return pl.BlockSpec((1, HEAD_DIM, s), lambda i, j, head=head: (i, head, 0))


def _pair_spec(s, pair):
    return pl.BlockSpec((1, 2 * HEAD_DIM, s), lambda i, j, pair=pair: (i, pair, 0))


def _merge_pairs(accs):
    lane = lax.broadcasted_iota(jnp.int32, accs[0].shape, 1)
    lo = lane < HEAD_DIM
    return jnp.concatenate([jnp.where(lo, accs[0], accs[1]), jnp.where(lo, accs[2], accs[3])], axis=1)


def _wide(x, width=TK):
    return jnp.concatenate([x] * (width // 128), axis=1)


def _online(s, mask, m, l, acc, vT):
    m_new = jnp.maximum(m, jnp.max(s, axis=1, keepdims=True))
    alpha = jnp.exp(m - m_new)
    p = jnp.exp(s - _wide(m_new, s.shape[1]))
    if mask is not None:
        p = jnp.where(mask, p, 0.0)
    l = alpha * l + jnp.sum(p, axis=1, keepdims=True)
    acc = alpha * acc + _dot_nt(p.astype(BF16), vT)
    return m_new, l, acc


def _softmax_init():
    return tuple((jnp.full((TQ, 128), NEG, F32), jnp.zeros((TQ, 128), F32), jnp.zeros((TQ, 128), F32))
                 for _ in range(HEADS))


def _finish(l, acc):
    return acc / jnp.maximum(l, 1e-30)


def _tile_iotas(qi):
    row = qi * TQ + lax.broadcasted_iota(jnp.int32, (TQ, TK), 0)
    lcol = lax.broadcasted_iota(jnp.int32, (TQ, TK), 1)
    return row, lcol


def _own_key_tile(qi):
    return (qi * TQ) // TK


def _sb_body(q_ref, k0_ref, k1_ref, k2_ref, k3_ref, v0_ref, v1_ref, u_ref, o_ref):
    qi = pl.program_id(1)
    k_refs = (k0_ref, k1_ref, k2_ref, k3_ref)
    v_refs = (v0_ref, v1_ref)
    row, lcol = _tile_iotas(qi)
    u = u_ref[...]
    qs = [q_ref[0, :, h * HEAD_DIM:(h + 1) * HEAD_DIM] for h in range(HEADS)]

    def tile(k0, before, accs, runs):
        new_a, new_r = [], []
        for h in range(HEADS):
            z = _dot(qs[h], k_refs[h][0, :, pl.ds(k0, TK)])
            sp = _softplus(z)
            lk = -sp if before is None else jnp.where(before, -sp, 0.0)
            hi, lo = _split2(lk)
            sums = _dot(hi, u) + _dot(lo, u)
            loc = sums[:, 0:TK]
            w = jnp.exp(z - sp + loc + _wide(runs[h]))
            if before is not None:
                w = jnp.where(before, w, 0.0)
            new_a.append(accs[h] + _dot_nt(w.astype(BF16), v_refs[h // 2][0, :, pl.ds(k0, TK)]))
            new_r.append(runs[h] + sums[:, TK:TK + 128])
        return tuple(new_a), tuple(new_r)

    zero_a = tuple(jnp.zeros((TQ, 128), F32) for _ in range(HEADS))
    zero_r = tuple(jnp.zeros((TQ, 128), F32) for _ in range(HEADS))
    kd = _own_key_tile(qi)
    q0 = pl.multiple_of(kd * TK, TK)
    accs, runs = tile(q0, (q0 + lcol) < row, zero_a, zero_r)

    def cond(c):
        jj, _, runs = c
        top = jnp.max(jnp.maximum(jnp.maximum(runs[0], runs[1]), jnp.maximum(runs[2], runs[3])))
        return jnp.logical_and(jj <= kd, top > SB_FLOOR)

    def body(c):
        jj, accs, runs = c
        accs, runs = tile(pl.multiple_of((kd - jj) * TK, TK), None, accs, runs)
        return jj + 1, accs, runs

    _, accs, _ = lax.while_loop(cond, body, (jnp.int32(1), accs, runs))
    o_ref[0] = _merge_pairs(accs).astype(BF16)


def _strict_upper_sum_matrix(n, total_cols=0):
    u = np.concatenate([np.tril(np.ones((n, n), np.float32), -1), np.ones((n, total_cols), np.float32)], axis=1)
    return jnp.asarray(u, BF16)


def sb_attention(q, kTb, vTb):
    b, s, _ = q.shape
    return pl.pallas_call(
        _sb_body,
        grid=(b, s // TQ),
        in_specs=[pl.BlockSpec((1, TQ, BRANCH_WIDTH), lambda i, j: (i, j, 0))]
        + [_head_spec(s, h) for h in range(HEADS)]
        + [_pair_spec(s, 0), _pair_spec(s, 1)]
        + [pl.BlockSpec((TK, TK + 128), lambda i, j: (0, 0))],
        out_specs=pl.BlockSpec((1, TQ, BRANCH_WIDTH), lambda i, j: (i, j, 0)),
        out_shape=jax.ShapeDtypeStruct((b, s, BRANCH_WIDTH), BF16),
        compiler_params=_cparams(("parallel", "arbitrary")),
        name="sb_attention",
    )(q, kTb, kTb, kTb, kTb, vTb, vTb, _strict_upper_sum_matrix(TK, 128))


def _fox_body(q_ref, k0_ref, k1_ref, k2_ref, k3_ref, v0_ref, v1_ref, cq_ref, ck_ref, o_ref):
    qi = pl.program_id(1)
    k_refs = (k0_ref, k1_ref, k2_ref, k3_ref)
    v_refs = (v0_ref, v1_ref)
    row, lcol = _tile_iotas(qi)
    qs = [q_ref[0, :, h * HEAD_DIM:(h + 1) * HEAD_DIM] for h in range(HEADS)]
    cqs = [jnp.broadcast_to(cq_ref[0, :, FGT_LANE0 + h:FGT_LANE0 + h + 1], (TQ, 128)) for h in range(HEADS)]

    def tile(k0, width, mask, carry):
        out = []
        for h in range(HEADS):
            m, l, acc = carry[h]
            z = _dot(qs[h], k_refs[h][0, :, pl.ds(k0, width)])
            s = z + _wide(cqs[h], width) - ck_ref[0, h:h + 1, pl.ds(k0, width)]
            if mask is not None:
                s = jnp.where(mask, s, NEG)
            out.append(_online(s, mask, m, l, acc, v_refs[h // 2][0, :, pl.ds(k0, width)]))
        return tuple(out)

    q0 = pl.multiple_of(_own_key_tile(qi) * TK, TK)
    n_wide = q0 // FOX_TK
    carry = lax.fori_loop(0, n_wide, lambda j, c: tile(pl.multiple_of(j * FOX_TK, FOX_TK), FOX_TK, None, c),
                          _softmax_init())
    mid0 = pl.multiple_of(n_wide * FOX_TK, FOX_TK)
    carry = lax.fori_loop(0, (q0 - mid0) // TK, lambda _, c: tile(mid0, TK, None, c), carry)
    carry = tile(q0, TK, (q0 + lcol) <= row, carry)
    o_ref[0] = _merge_pairs([_finish(l, acc) for (_, l, acc) in carry]).astype(BF16)


def fox_attention(q, kTb, vTb, cum, cumT):
    b, s, _ = q.shape
    return pl.pallas_call(
        _fox_body,
        grid=(b, s // TQ),
        in_specs=[pl.BlockSpec((1, TQ, BRANCH_WIDTH), lambda i, j: (i, j, 3))]
        + [_head_spec(s, FOX_KV0 + h) for h in range(HEADS)]
        + [_pair_spec(s, FOX_KV0 // 2), _pair_spec(s, FOX_KV0 // 2 + 1)]
        + [pl.BlockSpec((1, TQ, 128), lambda i, j: (i, j, 0)),
           pl.BlockSpec((1, 8, s), lambda i, j: (i, 0, 0))],
        out_specs=pl.BlockSpec((1, TQ, BRANCH_WIDTH), lambda i, j: (i, j, 0)),
        out_shape=jax.ShapeDtypeStruct((b, s, BRANCH_WIDTH), BF16),
        compiler_params=_cparams(("parallel", "arbitrary")),
        name="fox_attention",
    )(q, kTb, kTb, kTb, kTb, vTb, vTb, cum, cumT)


def _rank_select(score, n_cols, k):
    lane = lax.broadcasted_iota(jnp.int32, score.shape, 1)
    rank = jnp.zeros(score.shape, F32)
    for c in range(n_cols):
        col = score[:, c:c + 1]
        ahead = jnp.where(col > score, 1.0, jnp.where(col == score, jnp.where(lane > c, 1.0, 0.0), 0.0))
        rank = rank + ahead
    return jnp.where(rank < k, 1.0, 0.0)


def _rank_select_rows(score, n_cols, k):
    n_pad = -(-n_cols // 8) * 8
    st = score.T[0:n_pad, :]
    cand = lax.broadcasted_iota(jnp.int32, st.shape, 0)
    rank = jnp.zeros(st.shape, F32)
    for c in range(n_cols):
        ref = st[c:c + 1, :]
        ahead = jnp.where(ref > st, 1.0, jnp.where(ref == st, jnp.where(cand > c, 1.0, 0.0), 0.0))
        rank = rank + ahead
    sel = jnp.where(rank < k, 1.0, 0.0)
    if n_pad < 128:
        sel = jnp.concatenate([sel, jnp.zeros((128 - n_pad, st.shape[1]), F32)], axis=0)
    return sel.T


def _moba_body(q_ref, k0_ref, k1_ref, k2_ref, k3_ref, v0_ref, v1_ref, kmT_ref, em_ref, o_ref):
    qi = pl.program_id(1)
    k_refs = (k0_ref, k1_ref, k2_ref, k3_ref)
    v_refs = (v0_ref, v1_ref)
    row, lcol = _tile_iotas(qi)
    lane = lax.broadcasted_iota(jnp.int32, (TQ, 128), 1)
    n_blocks = k0_ref.shape[2] // MOBA_BLOCK
    kd = _own_key_tile(qi)
    qs = [q_ref[0, :, h * HEAD_DIM:(h + 1) * HEAD_DIM] for h in range(HEADS)]
    sels = []
    for h in range(HEADS):
        gate = _dot(qs[h], kmT_ref[0, h * HEAD_DIM:(h + 1) * HEAD_DIM, :].astype(BF16))
        gate = jnp.where(lane < kd, gate, NEG)
        sel = _rank_select_rows(gate, n_blocks, MOBA_TOPK)
        sels.append(jnp.where(lane < kd, sel, 0.0).astype(BF16))

    def past_tile(j, carry):
        k0 = pl.multiple_of(j * TK, TK)
        distf = (row - (k0 + lcol)).astype(F32)
        out = []
        for h in range(HEADS):
            m, l, acc = carry[h]
            z = _dot(qs[h], k_refs[h][0, :, pl.ds(k0, TK)])
            bit = _dot(sels[h], em_ref[:, pl.ds(pl.multiple_of(j * 128, 128), 128)])
            s = z - MOBA_SLOPES[h] * distf
            m_new = jnp.maximum(m, jnp.where(bit > 0.5, jnp.max(s, axis=1, keepdims=True), NEG))
            alpha = jnp.exp(m - m_new)
            p = jnp.exp(jnp.minimum(s - _wide(m_new), 0.0))
            l = alpha * l + bit * jnp.sum(p, axis=1, keepdims=True)
            acc = alpha * acc + bit * _dot_nt(p.astype(BF16), v_refs[h // 2][0, :, pl.ds(k0, TK)])
            out.append((m_new, l, acc))
        return tuple(out)

    def own_tile(carry):
        k0 = pl.multiple_of(kd * TK, TK)
        dist = row - (k0 + lcol)
        mask = dist >= 0
        distf = dist.astype(F32)
        out = []
        for h in range(HEADS):
            m, l, acc = carry[h]
            z = _dot(qs[h], k_refs[h][0, :, pl.ds(k0, TK)])
            s = jnp.where(mask, z - MOBA_SLOPES[h] * distf, NEG)
            out.append(_online(s, mask, m, l, acc, v_refs[h // 2][0, :, pl.ds(k0, TK)]))
        return tuple(out)

    carry = own_tile(lax.fori_loop(0, kd, past_tile, _softmax_init()))
    o_ref[0] = _merge_pairs([_finish(l, acc) for (_, l, acc) in carry]).astype(BF16)


def _block_expand_matrix(block, s):
    e = (np.arange(128)[:, None] == (np.arange(s)[None, :] // block)).astype(np.float32)
    return jnp.asarray(e, BF16)


def moba_attention(q, kTb, vTb, kmT):
    b, s, _ = q.shape
    n_blocks = s // MOBA_BLOCK
    return pl.pallas_call(
        _moba_body,
        grid=(b, s // TQ),
        in_specs=[pl.BlockSpec((1, TQ, BRANCH_WIDTH), lambda i, j: (i, j, 2))]
        + [_head_spec(s, MOBA_KV0 + h) for h in range(HEADS)]
        + [_pair_spec(s, MOBA_KV0 // 2), _pair_spec(s, MOBA_KV0 // 2 + 1)]
        + [pl.BlockSpec((1, BRANCH_WIDTH, 128), lambda i, j: (i, 0, 0)),
           pl.BlockSpec((128, n_blocks * 128), lambda i, j: (0, 0))],
        out_specs=pl.BlockSpec((1, TQ, BRANCH_WIDTH), lambda i, j: (i, j, 0)),
        out_shape=jax.ShapeDtypeStruct((b, s, BRANCH_WIDTH), BF16),
        compiler_params=_cparams(("parallel", "arbitrary")),
        name="moba_attention",
    )(q, kTb, kTb, kTb, kTb, vTb, vTb, kmT, _block_expand_matrix(128, n_blocks * 128))


def _nsa_body(q_ref, kT_ref, vT_ref, kcT_ref, vc_ref, kwT_ref, vwT_ref, g_ref, cover_ref, es_ref, o_ref, st_scr,
              *, n_cmp):
    qi = pl.program_id(1)
    s_len = kT_ref.shape[2]
    n_sel_blocks = s_len // SEL_BLOCK
    row, lcol = _tile_iotas(qi)
    qs = [q_ref[0, :, h * HEAD_DIM:(h + 1) * HEAD_DIM] for h in range(HEADS)]

    ncp = kcT_ref.shape[2]
    crow = qi * TQ + lax.broadcasted_iota(jnp.int32, (TQ, ncp), 0)
    cn = lax.broadcasted_iota(jnp.int32, (TQ, ncp), 1)
    dist_c = crow - (cn * CMP_STRIDE + (CMP_LEN - 1))
    mask_c = jnp.where(cn < n_cmp, dist_c, -1) >= 0
    dist_cf = dist_c.astype(F32)
    kcT = kcT_ref[0]
    vc = vc_ref[0]
    cover = cover_ref[...]
    o_c = []
    imp = jnp.zeros((TQ, 128), F32)
    for h in range(HEADS):
        s = jnp.where(mask_c, _dot(qs[h], kcT) - NSA_SLOPES[h] * dist_cf, NEG)
        m = jnp.max(s, axis=1, keepdims=True)
        p = jnp.where(mask_c, jnp.exp(s - m), 0.0)
        p = p / jnp.maximum(jnp.sum(p, axis=1, keepdims=True), 1e-30)
        pb = p.astype(BF16)
        o_c.append(_dot(pb, vc))
        imp = imp + _dot(pb, cover)

    lane = lax.broadcasted_iota(jnp.int32, (TQ, 128), 1)
    q_blk = (qi * TQ + lax.broadcasted_iota(jnp.int32, (TQ, 128), 0)) // SEL_BLOCK
    forced = (lane == 0) | (lane > q_blk - N_LOCAL_SEL)
    score = jnp.where(lane <= q_blk, jnp.where(forced, -NEG, imp), NEG)
    sel_f = jnp.where(lane <= q_blk, _rank_select_rows(score, n_sel_blocks, N_SEL), 0.0)
    sel = sel_f.astype(BF16)
    picked = jnp.max(sel_f, axis=0, keepdims=True)
    lane1 = lane[0:1, :]

    for h in range(HEADS):
        st_scr[0, h] = jnp.full((TQ, 128), NEG, F32)
        st_scr[1, h] = jnp.zeros((TQ, 128), F32)
        st_scr[2, h] = jnp.zeros((TQ, 128), F32)

    def sel_tile(k0, diag):
        kT = kT_ref[0, :, pl.ds(k0, TK)]
        vT = vT_ref[0, :, pl.ds(k0, TK)]
        dist = row - (k0 + lcol)
        selm = _dot(sel, es_ref[:, pl.ds(k0, TK)])
        if diag:
            mask = jnp.where(dist >= 0, selm, 0.0) > 0.5
        else:
            mask = selm > 0.5
        distf = dist.astype(F32)
        for h in range(HEADS):
            s = jnp.where(mask, _dot(qs[h], kT) - NSA_SLOPES[h] * distf, NEG)
            m, l, acc = _online(s, mask, st_scr[0, h], st_scr[1, h], st_scr[2, h], vT)
            st_scr[0, h] = m
            st_scr[1, h] = l
            st_scr[2, h] = acc

    def past_tile(j, _):
        blk0 = j * (TK // SEL_BLOCK)
        in_tile = (lane1 >= blk0) & (lane1 < blk0 + TK // SEL_BLOCK)
        hit = jnp.max(jnp.where(in_tile, picked, 0.0))

        @pl.when(hit > 0.5)
        def _():
            sel_tile(pl.multiple_of(j * TK, TK), False)

        return 0

    kd = _own_key_tile(qi)
    lax.fori_loop(0, kd, past_tile, 0)
    sel_tile(pl.multiple_of(kd * TK, TK), True)
    o_s = [_finish(st_scr[1, h], st_scr[2, h]) for h in range(HEADS)]

    def win_body(j, carry):
        k0 = pl.multiple_of(j * TK, TK)
        kT = kwT_ref[0, :, pl.ds(k0, TK)]
        vT = vwT_ref[0, :, pl.ds(k0, TK)]
        dist = row - (k0 + lcol)
        mask = jnp.where(dist >= 0, dist, WINDOW) < WINDOW
        distf = dist.astype(F32)
        out = []
        for h in range(HEADS):
            m, l, acc = carry[h]
            s = jnp.where(mask, _dot(qs[h], kT) - NSA_SLOPES[h] * distf, NEG)
            out.append(_online(s, mask, m, l, acc, vT))
        return tuple(out)

    res_w = lax.fori_loop(jnp.maximum(kd - WINDOW // TK, 0), kd + 1, win_body, _softmax_init())
    o_w = [_finish(l, acc)[:, 0:HEAD_DIM] for (_, l, acc) in res_w]

    g = _sigmoid(g_ref[0])
    outs = []
    for h in range(HEADS):
        g0 = g[:, 3 * h + 0:3 * h + 1]
        g1 = g[:, 3 * h + 1:3 * h + 2]
        g2 = g[:, 3 * h + 2:3 * h + 3]
        outs.append(g0 * o_c[h] + g1 * o_s[h][:, HEAD_DIM:2 * HEAD_DIM] + g2 * o_w[h])
    o_ref[0] = jnp.concatenate(outs, axis=1).astype(BF16)


def _cover_matrix():
    n = np.arange(256)[:, None] * CMP_STRIDE
    m = np.arange(128)[None, :] * SEL_BLOCK
    return jnp.asarray(((n < m + SEL_BLOCK) & (n + CMP_LEN > m)).astype(np.float32), BF16)


def nsa_attention(q, kTb, vTb, kcT, vc, kwT, vwT, misc, n_cmp):
    b, s, _ = q.shape
    ncp = kcT.shape[2]
    body = functools.partial(_nsa_body, n_cmp=n_cmp)
    return pl.pallas_call(
        body,
        grid=(b, s // TQ),
        in_specs=[
            pl.BlockSpec((1, TQ, BRANCH_WIDTH), lambda i, j: (i, j, 1)),
            _head_spec(s, SEL_KV),
            _pair_spec(s, SEL_KV // 2),
            pl.BlockSpec((1, HEAD_DIM, ncp), lambda i, j: (i, 0, 0)),
            pl.BlockSpec((1, ncp, HEAD_DIM), lambda i, j: (i, 0, 0)),
            pl.BlockSpec((1, HEAD_DIM, s), lambda i, j: (i, 0, 0)),
            pl.BlockSpec((1, 2 * HEAD_DIM, s), lambda i, j: (i, 0, 0)),
            pl.BlockSpec((1, TQ, 128), lambda i, j: (i, j, 1)),
            pl.BlockSpec((ncp, 128), lambda i, j: (0, 0)),
            pl.BlockSpec((128, s), lambda i, j: (0, 0)),
        ],
        out_specs=pl.BlockSpec((1, TQ, BRANCH_WIDTH), lambda i, j: (i, j, 0)),
        out_shape=jax.ShapeDtypeStruct((b, s, BRANCH_WIDTH), BF16),
        scratch_shapes=[pltpu.VMEM((3, HEADS, TQ, 128), F32)],
        compiler_params=_cparams(("parallel", "arbitrary")),
        name="nsa_attention",
    )(q, kTb, vTb, kcT, vc, kwT, vwT, misc, _cover_matrix()[:ncp], _block_expand_matrix(SEL_BLOCK, s))


def _merge_body(x_ref, h_ref, o0_ref, o1_ref, o2_ref, o3_ref, wb_ref, wg_ref, wo_ref, out_ref):
    h = h_ref[...]
    d = x_ref.shape[1]
    mix = None
    for n, o_ref in enumerate((o0_ref, o1_ref, o2_ref, o3_ref)):
        gate = _sigmoid(_dot(h, wg_ref[:, n * d:(n + 1) * d]))
        term = gate * _dot(o_ref[...], wb_ref[n])
        mix = term if mix is None else mix + term
    out_ref[...] = x_ref[...] + _dot(mix.astype(BF16), wo_ref[...])


def merge_branches(x, h, outs, wb, wg, wo, *, tm):
    m, d = x.shape
    row = lambda i: (i, 0)
    return pl.pallas_call(
        _merge_body,
        grid=(m // tm,),
        in_specs=[
            pl.BlockSpec((tm, d), row),
            pl.BlockSpec((tm, d), row),
            pl.BlockSpec((tm, BRANCH_WIDTH), row),
            pl.BlockSpec((tm, BRANCH_WIDTH), row),
            pl.BlockSpec((tm, BRANCH_WIDTH), row),
            pl.BlockSpec((tm, BRANCH_WIDTH), row),
            pl.BlockSpec((4, BRANCH_WIDTH, d), lambda i: (0, 0, 0)),
            pl.BlockSpec((d, 4 * d), lambda i: (0, 0)),
            pl.BlockSpec((d, d), lambda i: (0, 0)),
        ],
        out_specs=pl.BlockSpec((tm, d), row),
        out_shape=jax.ShapeDtypeStruct((m, d), F32),
        compiler_params=_cparams(("parallel",)),
        name="merge_branches",
    )(x, h, *outs, wb, wg, wo)


def _split_w_in(w_in, b_fgt):
    wq = w_in[:, :D_MODEL].astype(BF16)
    wkT = w_in[:, D_MODEL:D_MODEL + KV_COLS].T.astype(BF16)
    wvT = w_in[:, D_MODEL + KV_COLS:D_MODEL + 2 * KV_COLS].T.astype(BF16)
    rest = w_in[:, D_MODEL + 2 * KV_COLS:]
    wm = jnp.pad(rest, ((0, 0), (0, MISC_COLS - rest.shape[1]))).astype(BF16)
    bias = jnp.zeros((1, 128), F32).at[0, FGT_LANE0:FGT_LANE0 + HEADS].set(b_fgt)
    return wq, wkT, wvT, wm, bias


def prompt_branches(x, g_mix, w_parts, cmp, k_all, v_all, *, layer, batch, seq, tm):
    wq, wkT, wvT, wm, bias = w_parts
    h, q, k_all, v_all, kTb, vTb, misc, cum, kmT = mixer_project_prompt(
        x, g_mix, wq, wkT, wvT, wm, bias, k_all, v_all, layer=layer, tm=tm, batch=batch, seq=seq)
    B, S = batch, seq
    q3 = q.reshape(B, S, D_MODEL)
    misc3 = misc.reshape(B, S, MISC_COLS)
    cum3 = cum.reshape(B, S, 128)

    o_sb = sb_attention(q3, kTb, vTb)

    pos_k, w1k, w2k, pos_v, w1v, w2v = cmp
    n16 = S // CMP_STRIDE
    cw = CMP_STRIDE * HEAD_DIM
    cmp_rows = slice(CMP_KV * HEAD_DIM, (CMP_KV + 1) * HEAD_DIM)
    xs = jnp.stack([jnp.swapaxes(k_all[layer, :, cmp_rows, :], 1, 2).reshape(B, n16, cw),
                    jnp.swapaxes(v_all[layer, :, cmp_rows, :], 1, 2).reshape(B, n16, cw)])
    kvc = nsa_compress(xs, jnp.stack([pos_k, pos_v]), jnp.stack([w1k, w1v]), jnp.stack([w2k, w2v]))
    kcT = jnp.swapaxes(kvc[0], 1, 2).astype(BF16)
    vc = kvc[1].astype(BF16)
    kwT = jnp.swapaxes(misc3[:, :, 0:HEAD_DIM], 1, 2).astype(BF16)
    vwT = jnp.swapaxes(misc3[:, :, HEAD_DIM:2 * HEAD_DIM], 1, 2).astype(BF16)
    vwT = jnp.pad(vwT, ((0, 0), (0, HEAD_DIM), (0, 0)))
    o_nsa = nsa_attention(q3, kTb, vTb, kcT, vc, kwT, vwT, misc3, n16 - 1)

    o_moba = moba_attention(q3, kTb, vTb, kmT)

    cumT = jnp.swapaxes(cum3[:, :, FGT_LANE0:FGT_LANE0 + HEADS], 1, 2)
    cumT = jnp.pad(cumT, ((0, 0), (0, 8 - HEADS), (0, 0)))
    o_fox = fox_attention(q3, kTb, vTb, cum3, cumT)

    outs = [o.reshape(B * S, BRANCH_WIDTH) for o in (o_sb, o_nsa, o_moba, o_fox)]
    wb_len = min(WINDOW, S)
    state = (misc3[:, :, 128 + FGT_LANE0:128 + FGT_LANE0 + HEADS],
             misc3[:, S - wb_len:, 0:HEAD_DIM].reshape(B, wb_len, 1, HEAD_DIM),
             misc3[:, S - wb_len:, HEAD_DIM:2 * HEAD_DIM].reshape(B, wb_len, 1, HEAD_DIM))
    return h, outs, state, k_all, v_all


def kv_heads(a_all):
    depth, b, _, s = a_all.shape
    return jnp.transpose(a_all.reshape(depth, b, N_KV_HEADS, HEAD_DIM, s), (0, 1, 4, 2, 3))


def prompt_mixers(x, g_mix, w_parts, cmp, wb, wg, wo, k_all, v_all, *, layer, batch, seq, tm):
    h, outs, state, k_all, v_all = prompt_branches(x, g_mix, w_parts, cmp, k_all, v_all,
                                                   layer=layer, batch=batch, seq=seq, tm=tm)
    return merge_branches(x, h, outs, wb, wg, wo, tm=tm), state, k_all, v_all


PAGE = 128
N_PAGES = 16
PAST = PAGE * N_PAGES
Q_ROWS = 32
SUM_CHUNK = 256
G_SB, G_NSA, G_MOBA, G_FOX = 0, 1, 2, 3


def _row_const(vals):
    r = lax.broadcasted_iota(jnp.int32, (8, 1), 0)
    out = jnp.zeros((8, 1), F32)
    for i, v in enumerate(vals):
        out = jnp.where(r == i, v, out)
    return out


def _suffix_sum_excl(x, u, n_split):
    n = x.shape[1] // SUM_CHUNK
    carry = jnp.zeros((x.shape[0], 1), F32)
    outs = [None] * n
    for c in reversed(range(n)):
        xc = x[:, c * SUM_CHUNK:(c + 1) * SUM_CHUNK]
        parts = _split3(xc) if n_split == 3 else _split2(xc)
        loc = _dot(parts[0], u)
        for p in parts[1:]:
            loc = loc + _dot(p, u)
        outs[c] = loc + carry
        carry = carry + (loc[:, 0:1] + xc[:, 0:1])
    return jnp.concatenate(outs, axis=1)


def _decode_body(pt_ref, *refs):
    kp = refs[0:N_PAGES]
    vp = refs[N_PAGES:2 * N_PAGES]
    lfp = refs[2 * N_PAGES:3 * N_PAGES]
    (qm_ref, qn_ref, kn_ref, vn_ref, hs_ref, wn_ref, wkT_ref, wvT_ref, pos_ref, w1_ref, w2_ref,
     u_ref, cover_ref, es_ref, o_ref, s_scr, p_scr, cmp_scr) = refs[3 * N_PAGES:]
    del pt_ref
    qm = qm_ref[0]
    lane8 = lax.broadcasted_iota(jnp.int32, (8, 128), 1)
    row8 = lax.broadcasted_iota(jnp.int32, (8, 128), 0)
    live = row8[:, 0:1] < HEADS
    moba_rows = slice(MOBA_KV0 * HEAD_DIM, (MOBA_KV0 + HEADS) * HEAD_DIM)
    cmp_rows = slice(CMP_KV * HEAD_DIM, (CMP_KV + 2) * HEAD_DIM)

    ksum = []
    for j in range(N_PAGES):
        kT = kp[j][0, 0]
        s_scr[:, j * PAGE:(j + 1) * PAGE] = _dot(qm, kT.astype(BF16))
        ksum.append(jnp.sum(kT[moba_rows, :], axis=1, keepdims=True))
        cmp_scr[0, j * PAGE:(j + 1) * PAGE, :] = kT[cmp_rows, :].T
        cmp_scr[1, j * PAGE:(j + 1) * PAGE, :] = vp[j][0, 0, cmp_rows, :].T
    kn = kn_ref[0].astype(BF16).astype(F32)
    z_new = jnp.sum(qm.astype(F32) * kn, axis=1, keepdims=True)

    pos_f = lax.broadcasted_iota(jnp.int32, (8, PAST), 1).astype(F32)
    dist_f = float(PAST) - pos_f
    u = u_ref[...]
    hs = hs_ref[0]

    z = s_scr[8 * G_SB:8 * G_SB + 8, :]
    sp = _softplus(z)
    later = _suffix_sum_excl(-sp, u, 2)
    p_scr[8 * G_SB:8 * G_SB + 8, :] = jnp.exp(z - sp + later).astype(BF16)
    pnew = [jnp.zeros((8, 1), F32)]
    inv = [jnp.ones((8, 1), F32)]

    kvc = []
    for t in range(2):
        a = jnp.zeros((PAGE, CMP_HIDDEN), F32)
        b = jnp.zeros((PAGE, CMP_HIDDEN), F32)
        for r in range(CMP_STRIDE):
            xr = cmp_scr[t, pl.ds(r, PAST // CMP_STRIDE, stride=CMP_STRIDE), :][:, 0:HEAD_DIM]
            a = a + _dot((xr + pos_ref[t, r:r + 1, :]).astype(BF16), w1_ref[t, r])
            b = b + _dot((xr + pos_ref[t, CMP_STRIDE + r:CMP_STRIDE + r + 1, :]).astype(BF16),
                         w1_ref[t, CMP_STRIDE + r])
        hid = a + pltpu.roll(b, PAGE - 1, 0)
        kvc.append(_dot(_gelu_tanh(hid).astype(BF16), w2_ref[t]).astype(BF16))
    kc, vc = kvc
    n_cmp = PAST // CMP_STRIDE - 1

    qn = qn_ref[0]
    slope_n = _row_const(NSA_SLOPES)
    dist_c = float(PAST - (CMP_LEN - 1)) - float(CMP_STRIDE) * lane8.astype(F32)
    mask_c = lane8 < n_cmp
    s_c = jnp.where(mask_c, _dot_nt(qn, kc) - slope_n * dist_c, NEG)
    p_c = jnp.where(mask_c, jnp.exp(s_c - jnp.max(s_c, axis=1, keepdims=True)), 0.0)
    p_c = p_c / jnp.maximum(jnp.sum(p_c, axis=1, keepdims=True), 1e-30)
    p_cb = jnp.where(live, p_c, 0.0).astype(BF16)
    o_c = _dot(p_cb, vc)
    imp = jnp.sum(_dot(p_cb, cover_ref[...]), axis=0, keepdims=True)

    q_blk = PAST // SEL_BLOCK
    lane1 = lane8[0:1, :]
    forced = (lane1 == 0) | (lane1 > q_blk - N_LOCAL_SEL)
    score = jnp.where(lane1 <= q_blk, jnp.where(forced, -NEG, imp), NEG)
    sel = _rank_select(jnp.broadcast_to(score, (8, 128)), q_blk + 1, N_SEL)
    sel = jnp.where(lane8 <= q_blk, sel, 0.0)
    selm = _dot(sel.astype(BF16), es_ref[...]) > 0.5
    z = s_scr[8 * G_NSA:8 * G_NSA + 8, :]
    s_s = jnp.where(selm, z - slope_n * dist_f, NEG)
    zn = z_new[8 * G_NSA:8 * G_NSA + 8, :]
    m = jnp.maximum(jnp.max(s_s, axis=1, keepdims=True), zn)
    p = jnp.where(selm, jnp.exp(s_s - m), 0.0)
    pn = jnp.exp(zn - m)
    p_scr[8 * G_NSA:8 * G_NSA + 8, :] = p.astype(BF16)
    pnew.append(pn)
    inv.append(1.0 / jnp.maximum(jnp.sum(p, axis=1, keepdims=True) + pn, 1e-30))

    wl = lax.broadcasted_iota(jnp.int32, (8, WINDOW), 1)
    mask_w = wl >= 1
    wn = wn_ref[0]
    s_w = jnp.where(mask_w, _dot(qn, wkT_ref[0, 0].astype(BF16)) - slope_n * (float(WINDOW) - wl.astype(F32)), NEG)
    z_wn = jnp.sum(qn.astype(F32) * wn[:, 0:HEAD_DIM].astype(BF16).astype(F32), axis=1, keepdims=True)
    m = jnp.maximum(jnp.max(s_w, axis=1, keepdims=True), z_wn)
    p = jnp.where(mask_w, jnp.exp(s_w - m), 0.0)
    pn = jnp.exp(z_wn - m)
    o_w = (_dot_nt(p.astype(BF16), wvT_ref[0, 0].astype(BF16))
           + pn * wn[:, HEAD_DIM:2 * HEAD_DIM].astype(BF16).astype(F32))
    o_w = o_w / jnp.maximum(jnp.sum(p, axis=1, keepdims=True) + pn, 1e-30)

    n_blk = PAST // MOBA_BLOCK
    per_blk = MOBA_BLOCK // PAGE
    lane_km = lax.broadcasted_iota(jnp.int32, (BRANCH_WIDTH, 128), 1)
    kmT = jnp.zeros((BRANCH_WIDTH, 128), F32)
    for n in range(n_blk):
        kmT = jnp.where(lane_km == n, sum(ksum[n * per_blk:(n + 1) * per_blk]) * (1.0 / MOBA_BLOCK), kmT)
    gate = _dot(qm_ref[0, 8 * G_MOBA:8 * G_MOBA + 8, moba_rows], kmT.astype(BF16))
    gate = jnp.where(lane8 < n_blk, gate, NEG)
    selb = _rank_select(gate, n_blk, MOBA_TOPK)
    z = s_scr[8 * G_MOBA:8 * G_MOBA + 8, :]
    slope_m = _row_const(MOBA_SLOPES)
    sel_cols = jnp.concatenate([jnp.broadcast_to(selb[:, n:n + 1], (8, MOBA_BLOCK)) for n in range(n_blk)], axis=1)
    selk = sel_cols > 0.5
    s_m = jnp.where(selk, z - slope_m * dist_f, NEG)
    zn = z_new[8 * G_MOBA:8 * G_MOBA + 8, :]
    m = jnp.maximum(jnp.max(s_m, axis=1, keepdims=True), zn)
    p = jnp.where(selk, jnp.exp(s_m - m), 0.0)
    pn = jnp.exp(zn - m)
    p_scr[8 * G_MOBA:8 * G_MOBA + 8, :] = p.astype(BF16)
    pnew.append(pn)
    inv.append(1.0 / jnp.maximum(jnp.sum(p, axis=1, keepdims=True) + pn, 1e-30))

    lf = jnp.concatenate([pg[0, 0] for pg in lfp], axis=1)
    bias = _suffix_sum_excl(lf, u, 3) + hs[:, 3:4]
    z = s_scr[8 * G_FOX:8 * G_FOX + 8, :]
    s_f = z + bias
    zn = z_new[8 * G_FOX:8 * G_FOX + 8, :]
    m = jnp.maximum(jnp.max(s_f, axis=1, keepdims=True), zn)
    p = jnp.exp(s_f - m)
    pn = jnp.exp(zn - m)
    p_scr[8 * G_FOX:8 * G_FOX + 8, :] = p.astype(BF16)
    pnew.append(pn)
    inv.append(1.0 / jnp.maximum(jnp.sum(p, axis=1, keepdims=True) + pn, 1e-30))

    res = jnp.concatenate(pnew, axis=0) * vn_ref[0].astype(BF16).astype(F32)
    for j in range(N_PAGES):
        res = res + _dot_nt(p_scr[:, j * PAGE:(j + 1) * PAGE], vp[j][0, 0].astype(BF16))
    res = res * jnp.concatenate(inv, axis=0)

    def head_lanes(g, kv):
        return res[8 * g:8 * g + 8, kv * HEAD_DIM:(kv + 1) * HEAD_DIM]

    gsig = _sigmoid(hs)
    o_nsa = gsig[:, 0:1] * o_c + gsig[:, 1:2] * head_lanes(G_NSA, SEL_KV) + gsig[:, 2:3] * o_w
    pieces = [head_lanes(G_SB, h)[h:h + 1, :] for h in range(HEADS)]
    pieces += [o_nsa[h:h + 1, :] for h in range(HEADS)]
    pieces += [head_lanes(G_MOBA, MOBA_KV0 + h)[h:h + 1, :] for h in range(HEADS)]
    pieces += [head_lanes(G_FOX, FOX_KV0 + h)[h:h + 1, :] for h in range(HEADS)]
    o_ref[0] = jnp.concatenate(pieces, axis=1).astype(BF16)


def _query_rows(q):
    b = q.shape[0]
    kv_of = {G_SB: lambda h: h, G_NSA: lambda h: SEL_KV, G_MOBA: lambda h: MOBA_KV0 + h, G_FOX: lambda h: FOX_KV0 + h}
    rows = []
    for g in range(4):
        for h in range(HEADS):
            kv = kv_of[g](h)
            qh = q[:, (g * HEADS + h) * HEAD_DIM:(g * HEADS + h + 1) * HEAD_DIM]
            rows.append(jnp.pad(qh, ((0, 0), (kv * HEAD_DIM, (N_KV_HEADS - 1 - kv) * HEAD_DIM))))
        rows += [jnp.zeros((b, KV_COLS), q.dtype)] * (8 - HEADS)
    return jnp.stack(rows, axis=1)


def decode_attention(q, k_new, v_new, misc, ckT, cvT, clf, wkT, wvT, page_table, cmp, layer):
    b = q.shape[0]
    pos_k, w1k, w2k, pos_v, w1v, w2v = cmp
    qm = _query_rows(q)
    qn = jnp.pad(q[:, BRANCH_WIDTH:2 * BRANCH_WIDTH].reshape(b, HEADS, HEAD_DIM), ((0, 0), (0, 8 - HEADS), (0, 0)))
    gates = misc[:, 128:128 + 3 * HEADS].reshape(b, HEADS, 3)
    lf_new = misc[:, 128 + FGT_LANE0:128 + FGT_LANE0 + HEADS].reshape(b, HEADS, 1)
    hs = jnp.pad(jnp.concatenate([gates, lf_new], axis=2), ((0, 0), (0, 8 - HEADS), (0, 124)))
    wn = misc[:, 0:128].reshape(b, 1, 128)
    pos = jnp.stack([pos_k, pos_v])
    w1 = jnp.stack([w1k, w1v]).reshape(2, CMP_LEN, HEAD_DIM, CMP_HIDDEN).astype(BF16)
    w2 = jnp.stack([w2k, w2v]).astype(BF16)

    def page_map(j):
        return lambda i, pt: (layer, pt[i, j], 0, 0)

    per_b3 = lambda i, pt: (i, 0, 0)
    per_b4 = lambda i, pt: (layer, i, 0, 0)
    const2 = lambda i, pt: (0, 0)
    const3 = lambda i, pt: (0, 0, 0)
    const4 = lambda i, pt: (0, 0, 0, 0)
    in_specs = ([pl.BlockSpec((1, 1, KV_COLS, PAGE), page_map(j)) for j in range(N_PAGES)]
                + [pl.BlockSpec((1, 1, KV_COLS, PAGE), page_map(j)) for j in range(N_PAGES)]
                + [pl.BlockSpec((1, 1, 8, PAGE), page_map(j)) for j in range(N_PAGES)]
                + [pl.BlockSpec((1, Q_ROWS, KV_COLS), per_b3),
                   pl.BlockSpec((1, 8, HEAD_DIM), per_b3),
                   pl.BlockSpec((1, 1, KV_COLS), per_b3),
                   pl.BlockSpec((1, 1, KV_COLS), per_b3),
                   pl.BlockSpec((1, 8, 128), per_b3),
                   pl.BlockSpec((1, 1, 128), per_b3),
                   pl.BlockSpec((1, 1, HEAD_DIM, WINDOW), per_b4),
                   pl.BlockSpec((1, 1, HEAD_DIM, WINDOW), per_b4),
                   pl.BlockSpec((2, CMP_LEN, HEAD_DIM), const3),
                   pl.BlockSpec((2, CMP_LEN, HEAD_DIM, CMP_HIDDEN), const4),
                   pl.BlockSpec((2, CMP_HIDDEN, HEAD_DIM), const3),
                   pl.BlockSpec((SUM_CHUNK, SUM_CHUNK), const2),
                   pl.BlockSpec((128, 128), const2),
                   pl.BlockSpec((128, PAST), const2)])
    grid_spec = pltpu.PrefetchScalarGridSpec(
        num_scalar_prefetch=1,
        grid=(b,),
        in_specs=in_specs,
        out_specs=pl.BlockSpec((1, 1, D_MODEL), per_b3),
        scratch_shapes=[pltpu.VMEM((Q_ROWS, PAST), F32), pltpu.VMEM((Q_ROWS, PAST), BF16),
                        pltpu.VMEM((2, PAST, 128), F32)],
    )
    out = pl.pallas_call(
        _decode_body,
        grid_spec=grid_spec,
        out_shape=jax.ShapeDtypeStruct((b, 1, D_MODEL), BF16),
        compiler_params=_cparams(("arbitrary",)),
        name="decode_attention",
    )(page_table, *([ckT] * N_PAGES), *([cvT] * N_PAGES), *([clf] * N_PAGES),
      qm, qn, k_new.reshape(b, 1, KV_COLS), v_new.reshape(b, 1, KV_COLS), hs, wn, wkT, wvT,
      pos, w1, w2, _strict_upper_sum_matrix(SUM_CHUNK), _cover_matrix()[:128], _block_expand_matrix(SEL_BLOCK, PAST))
    return out.reshape(b, D_MODEL)


def sample_mixers(x, g_mix, w_parts, cmp, wb, wg, wo, ckT, cvT, clf, wkT, wvT, wk_buf, wv_buf, page_table, layer):
    b = x.shape[0]
    wq, wkT_w, wvT_w, wm, bias = w_parts
    h, q, k, v, misc = mixer_project_sample(x, g_mix, wq, wkT_w, wvT_w, wm, bias)
    o = decode_attention(q, k, v, misc, ckT, cvT, clf, wkT, wvT, page_table, cmp, layer)
    outs = [o[:, i * BRANCH_WIDTH:(i + 1) * BRANCH_WIDTH] for i in range(4)]
    x = merge_branches(x, h, outs, wb, wg, wo, tm=b)
    kw = misc[:, 0:HEAD_DIM].reshape(b, 1, 1, HEAD_DIM)
    vw = misc[:, HEAD_DIM:2 * HEAD_DIM].reshape(b, 1, 1, HEAD_DIM)
    state = (k.reshape(b, 1, N_KV_HEADS, HEAD_DIM), v.reshape(b, 1, N_KV_HEADS, HEAD_DIM),
             misc[:, 128 + FGT_LANE0:128 + FGT_LANE0 + HEADS].reshape(b, 1, HEADS),
             jnp.concatenate([wk_buf[:, 1:], kw], axis=1),
             jnp.concatenate([wv_buf[:, 1:], vw], axis=1))
    return x, state


PROMPT_TM = 512


def kernel(x_prompt, x_sample, cache_k, cache_v, cache_logf, state_win_k, state_win_v, page_table,
           g_ffn1, w_ffn1_gate, w_ffn1_up, w_ffn1_down, g_mix, w_in, b_fgt,
           cmp_pos_k, cmp_w1_k, cmp_w2_k, cmp_pos_v, cmp_w1_v, cmp_w2_v,
           w_branch, w_merge_gate, w_out, g_ffn2, w_ffn2_gate, w_ffn2_up, w_ffn2_down, g_final):
    B, S, d = x_prompt.shape
    depth = g_ffn1.shape[0]
    nb = x_sample.shape[0]
    n_phys = cache_k.shape[1]
    xp = x_prompt.reshape(B * S, d)
    xs = x_sample.reshape(nb, d)
    pst = [[] for _ in range(3)]
    sst = [[] for _ in range(5)]
    pk_all = jnp.zeros((depth, B, KV_COLS, S), F32)
    pv_all = jnp.zeros((depth, B, KV_COLS, S), F32)
    ckT = jnp.transpose(cache_k, (0, 1, 3, 4, 2)).reshape(depth, n_phys, KV_COLS, PAGE)
    cvT = jnp.transpose(cache_v, (0, 1, 3, 4, 2)).reshape(depth, n_phys, KV_COLS, PAGE)
    clf = jnp.pad(jnp.swapaxes(cache_logf, 2, 3), ((0, 0), (0, 0), (0, 8 - HEADS), (0, 0)))
    wkT = jnp.transpose(state_win_k, (0, 1, 3, 4, 2)).reshape(depth, nb, HEAD_DIM, WINDOW)
    wvT = jnp.transpose(state_win_v, (0, 1, 3, 4, 2)).reshape(depth, nb, HEAD_DIM, WINDOW)
    for l in range(depth):
        last = l == depth - 1
        f1 = (w_ffn1_gate[l].astype(BF16), w_ffn1_up[l].astype(BF16), w_ffn1_down[l].astype(BF16))
        f2 = (w_ffn2_gate[l].astype(BF16), w_ffn2_up[l].astype(BF16), w_ffn2_down[l].astype(BF16))
        w_parts = _split_w_in(w_in[l], b_fgt[l])
        cmp = (cmp_pos_k[l], cmp_w1_k[l], cmp_w2_k[l], cmp_pos_v[l], cmp_w1_v[l], cmp_w2_v[l])
        wb, wg, wo = w_branch[l].astype(BF16), w_merge_gate[l].astype(BF16), w_out[l].astype(BF16)
        xp = ffn_half(xp, g_ffn1[l], *f1, tm=PROMPT_TM)
        xp, st_p, pk_all, pv_all = prompt_mixers(xp, g_mix[l], w_parts, cmp, wb, wg, wo, pk_all, pv_all,
                                                 layer=l, batch=B, seq=S, tm=PROMPT_TM)
        xp = ffn_half(xp, g_ffn2[l], *f2, g_final if last else None, tm=PROMPT_TM)
        xs = ffn_half(xs, g_ffn1[l], *f1, tm=nb)
        xs, st_s = sample_mixers(xs, g_mix[l], w_parts, cmp, wb, wg, wo, ckT, cvT, clf, wkT, wvT,
                                 state_win_k[l], state_win_v[l], page_table, l)
        xs = ffn_half(xs, g_ffn2[l], *f2, g_final if last else None, tm=nb)
        for i in range(3):
            pst[i].append(st_p[i])
        for i in range(5):
            sst[i].append(st_s[i])
    y_prompt = xp.reshape(B, S, d)
    y_sample = xs.reshape(nb, 1, d)
    return ((y_prompt, y_sample, kv_heads(pk_all), kv_heads(pv_all)) + tuple(jnp.stack(p) for p in pst)
            + tuple(jnp.stack(p) for p in sst))
```
